```python
import math
import jax, jax.numpy as jnp
from jax import lax
import numpy as np

D_MODEL = 2048
BATCH = 4
SEQ = 2048
DEPTH = 1
DEC_BATCH = 32
DEC_SEQ = 4
PAST_LEN = 8192
PAGE_SIZE = 128

RET_HEADS = 8
RET_DK = 128
RET_DV = 128
RET_CHUNK = 128
DIFF_HEADS = 4
DIFF_DH = 128
Q_BLOCK = 128
MEM_LEN = 256
X_HEADS = 4
X_DH = 128
PEER_HEADS = 8
PEER_NKEYS = 128
PEER_N = PEER_NKEYS * PEER_NKEYS
PEER_DKEY = 256
PEER_HALF = PEER_DKEY // 2
PEER_TOPK = 16
PEER_BLOCK = 256
ROPE_BASE = 10000.0
LAMBDA_INIT = 0.8 - 0.6 * math.exp(-0.3 * 0)
EPS = 1e-6

RET_W = RET_HEADS * RET_DK
RET_VW = RET_HEADS * RET_DV
DIFF_W = DIFF_HEADS * 2 * DIFF_DH
SPLIT_WIDTHS = (RET_W, RET_W, RET_VW, RET_VW, DIFF_W, DIFF_W, DIFF_W, D_MODEL, D_MODEL)

kernel_name = "hybrid_retention_diffattn_peer_step"

F32 = jnp.float32


def rmsnorm(x, g):
    xf = x.astype(F32)
    y = xf * lax.rsqrt(jnp.mean(xf * xf, axis=-1, keepdims=True) + EPS)
    return (y * g.astype(F32)).astype(x.dtype)


def head_rms(t):
    tf = t.astype(F32)
    return tf * lax.rsqrt(jnp.mean(tf * tf, axis=-1, keepdims=True) + EPS)


def rotary(x, pos):
    half = x.shape[-1] // 2
    freqs = jnp.exp(-math.log(ROPE_BASE) * jnp.arange(half, dtype=F32) / half)
    ang = pos.astype(F32)[:, None] * freqs[None, :]
    cos = jnp.cos(ang)[None, :, None, :]
    sin = jnp.sin(ang)[None, :, None, :]
    xf = x.astype(F32)
    x1, x2 = xf[..., :half], xf[..., half:]
    return jnp.concatenate([x1 * cos - x2 * sin, x1 * sin + x2 * cos], axis=-1).astype(x.dtype)


def ret_log_gamma():
    return jnp.log1p(-jnp.exp2(-5.0 - jnp.arange(RET_HEADS, dtype=F32)))


def retention_chunk(q, k, v, s_prev):
    q, k, v, s_prev = q.astype(F32), k.astype(F32), v.astype(F32), s_prev.astype(F32)
    c = q.shape[2]
    lg = ret_log_gamma()
    i = jnp.arange(c, dtype=F32)
    rel = i[:, None] - i[None, :]
    causal = rel >= 0
    dmat = jnp.where(causal[None], jnp.exp(jnp.where(causal, rel, 0.0)[None] * lg[:, None, None]), 0.0)
    dec_in = jnp.exp((i + 1.0)[None, :] * lg[:, None])
    dec_out = jnp.exp((c - 1.0 - i)[None, :] * lg[:, None])
    dec_chunk = jnp.exp(c * lg)
    scores = jnp.einsum('bhid,bhjd->bhij', q, k) * dmat[None]
    inner = jnp.einsum('bhij,bhjv->bhiv', scores, v)
    cross = jnp.einsum('bhid,bhdv->bhiv', q, s_prev) * dec_in[None, :, :, None]
    s_new = dec_chunk[None, :, None, None] * s_prev + jnp.einsum(
        'bhjd,bhjv->bhdv', k * dec_out[None, :, :, None], v)
    return inner + cross, s_new


def retention_prompt(q, k, v):
    b, s = q.shape[0], q.shape[1]
    n = s // RET_CHUNK

    def to_chunks(t):
        return t.reshape(b, n, RET_CHUNK, t.shape[2], t.shape[3]).transpose(1, 0, 3, 2, 4)

    s0 = jnp.zeros((b, RET_HEADS, RET_DK, RET_DV), F32)

    def step(carry, inp):
        o, s_new = retention_chunk(inp[0], inp[1], inp[2], carry)
        return s_new, o

    s_fin, o = lax.scan(step, s0, (to_chunks(q), to_chunks(k), to_chunks(v)))
    o = o.transpose(1, 0, 3, 2, 4).reshape(b, s, RET_HEADS, RET_DV)
    return o, s_fin


def diff_attn(q, ks, vs, masks, lam):
    scale = DIFF_DH ** -0.5
    scores = []
    for k, m in zip(ks, masks):
        s = jnp.einsum('bthcd,blhcd->bhctl', q, k).astype(F32) * scale
        if m is not None:
            s = jnp.where(m[None, None, None], s, -jnp.inf)
        scores.append(s)
    p = jax.nn.softmax(jnp.concatenate(scores, axis=-1), axis=-1)
    a = p[:, :, 0] - lam * p[:, :, 1]
    out = None
    off = 0
    for v in vs:
        l = v.shape[1]
        o = jnp.einsum('bhtl,blhe->bthe', a[..., off:off + l].astype(v.dtype), v)
        out = o if out is None else out + o
        off += l
    return out


def diff_prompt(q, k, v, lam):
    b, s = q.shape[0], q.shape[1]
    nb = s // Q_BLOCK
    qb = q.reshape(b, nb, Q_BLOCK, DIFF_HEADS, 2, DIFF_DH).swapaxes(0, 1)
    starts = jnp.arange(nb, dtype=jnp.int32) * Q_BLOCK
    kpos = jnp.arange(s, dtype=jnp.int32)

    def blk(args):
        qi, st = args
        qpos = st + jnp.arange(Q_BLOCK, dtype=jnp.int32)
        mask = qpos[:, None] >= kpos[None, :]
        return diff_attn(qi, [k], [v], [mask], lam)

    o = lax.map(blk, (qb, starts))
    return o.swapaxes(0, 1).reshape(b, s, DIFF_HEADS, 2 * DIFF_DH)


def mixer_in(x, pos, g_mix, w_in):
    b, t, _ = x.shape
    z = rmsnorm(x, g_mix) @ w_in
    points = [int(p) for p in np.cumsum(SPLIT_WIDTHS)[:-1]]
    rq, rk, rv, rg, dq, dk, dv, ga, gb = jnp.split(z, points, axis=-1)
    rq = rotary(rq.reshape(b, t, RET_HEADS, RET_DK), pos)
    rk = rotary(rk.reshape(b, t, RET_HEADS, RET_DK), pos) * (RET_DK ** -0.5)
    rv = rv.reshape(b, t, RET_HEADS, RET_DV)
    dq = dq.reshape(b, t, DIFF_HEADS, 2, DIFF_DH)
    dk = dk.reshape(b, t, DIFF_HEADS, 2, DIFF_DH)
    dv = dv.reshape(b, t, DIFF_HEADS, 2 * DIFF_DH)
    return rq, rk, rv, rg, dq, dk, dv, ga, gb


def mixer_out(x, ret_o, rg, diff_o, ga, gb, ret_norm_g, diff_norm_g, w_branch_a, w_branch_b, w_out):
    b, t, _ = x.shape
    ret_y = (head_rms(ret_o) * ret_norm_g.astype(F32).reshape(RET_HEADS, RET_DV)).reshape(b, t, RET_VW)
    ret_y = ret_y * jax.nn.silu(rg.astype(F32))
    diff_y = (head_rms(diff_o) * diff_norm_g.astype(F32) * (1.0 - LAMBDA_INIT)).reshape(b, t, DIFF_W)
    pa = (ret_y.astype(x.dtype) @ w_branch_a).astype(F32)
    pb = (diff_y.astype(x.dtype) @ w_branch_b).astype(F32)
    m = jax.nn.sigmoid(ga.astype(F32)) * pa + jax.nn.sigmoid(gb.astype(F32)) * pb
    return x + m.astype(x.dtype) @ w_out


def cross_attn(hn, mem_k, mem_v, w_xq, w_xo):
    b, t, _ = hn.shape
    q = (hn @ w_xq).reshape(b, t, X_HEADS, X_DH)
    s = jnp.einsum('bthd,bmhd->bhtm', q, mem_k).astype(F32) * (X_DH ** -0.5)
    p = jax.nn.softmax(s, axis=-1)
    o = jnp.einsum('bhtm,bmhd->bthd', p.astype(mem_v.dtype), mem_v).reshape(b, t, X_HEADS * X_DH)
    return o @ w_xo


def peer(xn, w_pq, peer_keys, peer_u, peer_v):
    t, d = xn.shape
    pad = (-t) % PEER_BLOCK
    xp = jnp.pad(xn, ((0, pad), (0, 0))).reshape(-1, PEER_BLOCK, d)

    def block(xb):
        q = (xb @ w_pq).reshape(PEER_BLOCK, PEER_HEADS, 2, PEER_HALF)
        s = jnp.einsum('thcd,chnd->thcn', q, peer_keys).astype(F32)
        s1, i1 = lax.top_k(s[:, :, 0], PEER_TOPK)
        s2, i2 = lax.top_k(s[:, :, 1], PEER_TOPK)
        cand = (s1[..., :, None] + s2[..., None, :]).reshape(PEER_BLOCK, PEER_HEADS, PEER_TOPK * PEER_TOPK)
        cidx = (i1[..., :, None] * PEER_NKEYS + i2[..., None, :]).reshape(PEER_BLOCK, PEER_HEADS, PEER_TOPK * PEER_TOPK)
        top, sel = lax.top_k(cand, PEER_TOPK)
        eidx = jnp.take_along_axis(cidx, sel, axis=-1)
        g = jax.nn.softmax(top, axis=-1)
        u = peer_u[eidx]
        hact = jax.nn.gelu(jnp.einsum('td,thkd->thk', xb, u).astype(F32))
        w = (g * hact).astype(xb.dtype)
        return jnp.einsum('thk,thkd->td', w, peer_v[eidx])

    out = lax.map(block, xp).reshape(-1, d)
    return out[:t]


def tail(h1, mem_k, mem_v, g_cross, w_xq, w_xo, g_ffn, w_pq, peer_keys, peer_u, peer_v, g_final):
    h2 = h1 + cross_attn(rmsnorm(h1, g_cross), mem_k, mem_v, w_xq, w_xo)
    b, t, d = h2.shape
    f = peer(rmsnorm(h2, g_ffn).reshape(b * t, d), w_pq, peer_keys, peer_u, peer_v).reshape(b, t, d)
    return rmsnorm(h2 + f, g_final)


def setup_inputs(seed: int = 0) -> dict:
    key = jax.random.key(seed)
    ks = jax.random.split(key, 32)

    def nrm(i, shape, scale):
        return jax.random.normal(ks[i], shape, F32) * scale

    n_pages = PAST_LEN // PAGE_SIZE
    n_used = DEC_BATCH * n_pages
    n_pool = n_used + max(1, n_used // 4)
    page_table = jax.random.permutation(ks[7], n_pool)[:n_used].reshape(DEC_BATCH, n_pages).astype(jnp.int32)
    in_w = sum(SPLIT_WIDTHS)
    return {
        "x_prompt": nrm(0, (BATCH, SEQ, D_MODEL), 1.0),
        "x_sample": nrm(1, (DEC_BATCH, DEC_SEQ, D_MODEL), 1.0),
        "cache_k": nrm(2, (n_pool, PAGE_SIZE, DIFF_HEADS, 2 * DIFF_DH), 1.0),
        "cache_v": nrm(3, (n_pool, PAGE_SIZE, DIFF_HEADS, 2 * DIFF_DH), 1.0),
        "state_ret": nrm(4, (DEC_BATCH, RET_HEADS, RET_DK, RET_DV), 0.3),
        "cache_mem_k": nrm(5, (DEC_BATCH, MEM_LEN, X_HEADS, X_DH), 1.0),
        "cache_mem_v": nrm(6, (DEC_BATCH, MEM_LEN, X_HEADS, X_DH), 1.0),
        "page_table": page_table,
        "mem_prompt": nrm(8, (BATCH, MEM_LEN, D_MODEL), 1.0),
        "g_mix": 1.0 + nrm(9, (D_MODEL,), 0.02),
        "w_in": nrm(10, (D_MODEL, in_w), D_MODEL ** -0.5),
        "ret_norm_g": 1.0 + nrm(11, (RET_VW,), 0.02),
        "diff_norm_g": 1.0 + nrm(12, (2 * DIFF_DH,), 0.02),
        "lambda_q1": nrm(13, (DIFF_DH,), 0.1),
        "lambda_k1": nrm(14, (DIFF_DH,), 0.1),
        "lambda_q2": nrm(15, (DIFF_DH,), 0.1),
        "lambda_k2": nrm(16, (DIFF_DH,), 0.1),
        "w_branch_a": nrm(17, (RET_VW, D_MODEL), RET_VW ** -0.5),
        "w_branch_b": nrm(18, (DIFF_W, D_MODEL), DIFF_W ** -0.5),
        "w_out": nrm(19, (D_MODEL, D_MODEL), D_MODEL ** -0.5),
        "g_cross": 1.0 + nrm(20, (D_MODEL,), 0.02),
        "w_xq": nrm(21, (D_MODEL, X_HEADS * X_DH), D_MODEL ** -0.5),
        "w_mem_kv": nrm(22, (D_MODEL, 2 * X_HEADS * X_DH), D_MODEL ** -0.5),
        "w_xo": nrm(23, (X_HEADS * X_DH, D_MODEL), (X_HEADS * X_DH) ** -0.5),
        "g_ffn": 1.0 + nrm(24, (D_MODEL,), 0.02),
        "w_pq": nrm(25, (D_MODEL, PEER_HEADS * PEER_DKEY), D_MODEL ** -0.5),
        "peer_keys": nrm(26, (2, PEER_HEADS, PEER_NKEYS, PEER_HALF), PEER_HALF ** -0.5),
        "peer_u": nrm(27, (PEER_N, D_MODEL), D_MODEL ** -0.5),
        "peer_v": nrm(28, (PEER_N, D_MODEL), PEER_HEADS ** -0.5),
        "g_final": 1.0 + nrm(29, (D_MODEL,), 0.02),
    }


def reference(x_prompt, x_sample, cache_k, cache_v, state_ret, cache_mem_k, cache_mem_v, page_table,
              mem_prompt, g_mix, w_in, ret_norm_g, diff_norm_g, lambda_q1, lambda_k1, lambda_q2, lambda_k2,
              w_branch_a, w_branch_b, w_out, g_cross, w_xq, w_mem_kv, w_xo, g_ffn, w_pq, peer_keys,
              peer_u, peer_v, g_final):
    lam = (jnp.exp(jnp.sum(lambda_q1.astype(F32) * lambda_k1.astype(F32)))
           - jnp.exp(jnp.sum(lambda_q2.astype(F32) * lambda_k2.astype(F32))) + LAMBDA_INIT)

    bp, sp, _ = x_prompt.shape
    pos_p = jnp.arange(sp, dtype=jnp.int32)
    rq, rk, rv, rg, dq, dk, dv, ga, gb = mixer_in(x_prompt, pos_p, g_mix, w_in)
    ret_o, ret_state_prompt = retention_prompt(rq, rk, rv)
    diff_o = diff_prompt(dq, dk, dv, lam)
    h1 = mixer_out(x_prompt, ret_o, rg, diff_o, ga, gb, ret_norm_g, diff_norm_g, w_branch_a, w_branch_b, w_out)
    mem_kv = (mem_prompt @ w_mem_kv).reshape(bp, MEM_LEN, 2, X_HEADS, X_DH)
    mem_k_prompt = mem_kv[:, :, 0]
    mem_v_prompt = mem_kv[:, :, 1]
    y_prompt = tail(h1, mem_k_prompt, mem_v_prompt, g_cross, w_xq, w_xo, g_ffn, w_pq, peer_keys,
                    peer_u, peer_v, g_final)
    k_prompt = dk.reshape(bp, sp, DIFF_HEADS, 2 * DIFF_DH)
    v_prompt = dv

    bd, td, _ = x_sample.shape
    n_pages = page_table.shape[1]
    past = n_pages * PAGE_SIZE
    pos_s = past + jnp.arange(td, dtype=jnp.int32)
    sq, sk, sv, sg, eq, ek, ev, sga, sgb = mixer_in(x_sample, pos_s, g_mix, w_in)
    ret_o_s, ret_state_sample = retention_chunk(sq.transpose(0, 2, 1, 3), sk.transpose(0, 2, 1, 3),
                                                sv.transpose(0, 2, 1, 3), state_ret)
    ret_o_s = ret_o_s.transpose(0, 2, 1, 3)
    past_k = cache_k[page_table].reshape(bd, past, DIFF_HEADS, 2, DIFF_DH)
    past_v = cache_v[page_table].reshape(bd, past, DIFF_HEADS, 2 * DIFF_DH)
    tpos = jnp.arange(td, dtype=jnp.int32)
    new_mask = tpos[:, None] >= tpos[None, :]
    diff_o_s = diff_attn(eq, [past_k.astype(ek.dtype), ek], [past_v.astype(ev.dtype), ev], [None, new_mask], lam)
    h1s = mixer_out(x_sample, ret_o_s, sg, diff_o_s, sga, sgb, ret_norm_g, diff_norm_g, w_branch_a, w_branch_b, w_out)
    y_sample = tail(h1s, cache_mem_k.astype(x_sample.dtype), cache_mem_v.astype(x_sample.dtype), g_cross, w_xq,
                    w_xo, g_ffn, w_pq, peer_keys, peer_u, peer_v, g_final)
    k_sample = ek.reshape(bd, td, DIFF_HEADS, 2 * DIFF_DH)
    v_sample = ev

    return (y_prompt, y_sample, k_prompt, v_prompt, ret_state_prompt, mem_k_prompt, mem_v_prompt,
            k_sample, v_sample, ret_state_sample)
```

```python
import functools
import math

import numpy as np
import jax
import jax.numpy as jnp
from jax import lax
from jax.experimental import pallas as pl
from jax.experimental.pallas import tpu as pltpu

F32 = jnp.float32
BF16 = jnp.bfloat16

D_MODEL = 2048
RET_HEADS = 8
RET_DK = 128
RET_DV = 128
RET_CHUNK = 128
DIFF_HEADS = 4
DIFF_DH = 128
DIFF_HW = 2 * DIFF_DH
PAGE_SIZE = 128
MEM_LEN = 256
X_HEADS = 4
X_DH = 128
PEER_HEADS = 8
PEER_NKEYS = 128
PEER_HALF = 128
PEER_TOPK = 16
ROPE_BASE = 10000.0
LAMBDA_INIT = 0.8 - 0.6 * math.exp(-0.3 * 0)
EPS = 1e-6

RET_W = RET_HEADS * RET_DK
DIFF_W = DIFF_HEADS * DIFF_HW
OFF_RQ, OFF_RK, OFF_RV, OFF_RG = 0, RET_W, 2 * RET_W, 3 * RET_W
OFF_DQ = 4 * RET_W
OFF_DK = OFF_DQ + DIFF_W
OFF_DV = OFF_DK + DIFF_W
OFF_GA = OFF_DV + DIFF_W
OFF_GB = OFF_GA + D_MODEL
IN_W = OFF_GB + D_MODEL

LANES = 128
VMEM_LIMIT = 48 * 1024 * 1024
PEER_VMEM_LIMIT = 58 * 1024 * 1024

NEG_INF = float("-inf")


def _cparams(*sem, vmem=VMEM_LIMIT):
    return pltpu.CompilerParams(dimension_semantics=sem, vmem_limit_bytes=vmem)


def _dot(a, b):
    return jnp.dot(a, b, preferred_element_type=F32)


def _dot_nt(a, b):
    return lax.dot_general(a, b, (((1,), (1,)), ((), ())), preferred_element_type=F32)


def _dot_tn(a, b):
    return lax.dot_general(a, b, (((0,), (0,)), ((), ())), preferred_element_type=F32)


def _mm_kernel(*refs, norm, residual):
    it = iter(refs)
    x_ref = next(it)
    g_ref = next(it) if norm else None
    w_ref = next(it)
    r_ref = next(it) if residual else None
    o_ref = next(it)
    xs_ref = next(it)

    @pl.when(pl.program_id(1) == 0)
    def _():
        x = x_ref[...].astype(F32)
        if norm:
            ms = jnp.mean(x * x, axis=-1, keepdims=True)
            x = x * lax.rsqrt(ms + EPS) * g_ref[...]
        xs_ref[...] = x.astype(BF16)

    acc = _dot(xs_ref[...], w_ref[...])
    if residual:
        acc = r_ref[...] + acc
    o_ref[...] = acc.astype(o_ref.dtype)


def _matmul(x, w, *, gain=None, residual=None, out_dtype=F32, bm=1024, bn=512):
    m, k = x.shape
    n = w.shape[1]
    bm = min(bm, m)
    bn = min(bn, n)
    assert m % bm == 0 and n % bn == 0
    norm = gain is not None
    res = residual is not None
    in_specs = [pl.BlockSpec((bm, k), lambda i, j: (i, 0))]
    args = [x]
    if norm:
        in_specs.append(pl.BlockSpec((1, k), lambda i, j: (0, 0)))
        args.append(gain.reshape(1, k).astype(F32))
    in_specs.append(pl.BlockSpec((k, bn), lambda i, j: (0, j)))
    args.append(w)
    if res:
        in_specs.append(pl.BlockSpec((bm, bn), lambda i, j: (i, j)))
        args.append(residual)
    return pl.pallas_call(
        functools.partial(_mm_kernel, norm=norm, residual=res),
        grid=(m // bm, n // bn),
        in_specs=in_specs,
        out_specs=pl.BlockSpec((bm, bn), lambda i, j: (i, j)),
        out_shape=jax.ShapeDtypeStruct((m, n), out_dtype),
        scratch_shapes=[pltpu.VMEM((bm, k), BF16)],
        compiler_params=_cparams("parallel", "arbitrary"),
    )(*args)


def _ret_tables(chunk_len):
    c = RET_CHUNK
    lg = jnp.log1p(-jnp.exp2(-5.0 - jnp.arange(RET_HEADS, dtype=F32)))
    i = jnp.arange(c, dtype=F32)
    rel = i[:, None] - i[None, :]
    causal = rel >= 0
    dmat = jnp.where(causal[None], jnp.exp(jnp.where(causal, rel, 0.0)[None] * lg[:, None, None]), 0.0)
    dec_in = jnp.exp((i + 1.0)[None, :] * lg[:, None])
    dec_out = jnp.exp((chunk_len - 1.0 - i)[None, :] * lg[:, None])
    dec_chunk = jnp.exp(chunk_len * lg)
    ones = jnp.ones((RET_HEADS, c, LANES), F32)
    return (dmat, dec_in[:, :, None] * ones, dec_out[:, :, None] * ones,
            dec_chunk[:, None, None] * ones)


def _rope_tables(pos):
    half = RET_DK // 2
    freqs = jnp.exp(-math.log(ROPE_BASE) * jnp.arange(half, dtype=F32) / half)
    ang = pos.astype(F32)[:, None] * freqs[None, :]
    cos, sin = jnp.cos(ang), jnp.sin(ang)
    return jnp.concatenate([cos, cos], axis=-1), jnp.concatenate([-sin, sin], axis=-1)


def _ret_kernel(*refs, rows, has_state):
    it = iter(refs)
    q_ref, k_ref, v_ref, cos_ref, sin_ref = (next(it) for _ in range(5))
    dmat_ref, din_ref, dout_ref, dch_ref = (next(it) for _ in range(4))
    s0_ref = next(it) if has_state else None
    o_ref, sfin_ref, s_ref = next(it), next(it), next(it)
    pad_ref = next(it) if rows < RET_CHUNK else None
    n = pl.program_id(2)

    @pl.when(n == 0)
    def _():
        if has_state:
            s_ref[...] = s0_ref[0, 0]
        else:
            s_ref[...] = jnp.zeros_like(s_ref)

    def full(x):
        if rows == RET_CHUNK:
            return x
        pad_ref[...] = jnp.zeros_like(pad_ref)
        pad_ref[0:rows, :] = x
        return pad_ref[...]

    def rot(x, cos, sin):
        return x * cos + pltpu.roll(x, RET_DK // 2, axis=1) * sin

    cos, sin = full(cos_ref[...]), full(sin_ref[...])
    q = rot(full(q_ref[0]), cos, sin)
    k = rot(full(k_ref[0]), cos, sin) * (RET_DK ** -0.5)
    v = full(v_ref[0]).astype(BF16)
    s_prev = s_ref[...]
    qb = q.astype(BF16)
    scores = _dot_nt(qb, k.astype(BF16)) * dmat_ref[0]
    inner = _dot(scores.astype(BF16), v)
    cross = _dot(qb, s_prev.astype(BF16)) * din_ref[0]
    out = inner + cross
    s_new = dch_ref[0] * s_prev + _dot_tn((k * dout_ref[0]).astype(BF16), v)
    s_ref[...] = s_new
    if rows == RET_CHUNK:
        o_ref[0] = out
    else:
        o_ref[0] = out[0:rows, :]

    @pl.when(n == pl.num_programs(2) - 1)
    def _():
        sfin_ref[0, 0] = s_new


def _retention(z3, pos, state0, chunk_len):
    b, s, _ = z3.shape
    rows = min(s, RET_CHUNK)
    nchunks = s // rows
    cos, sin = _rope_tables(pos)
    dmat, din, dout, dch = _ret_tables(float(chunk_len))
    has_state = state0 is not None
    qkv_spec = lambda off: pl.BlockSpec((1, rows, RET_DK), lambda bi, h, n: (bi, n, off // RET_DK + h))
    tab_spec = pl.BlockSpec((1, RET_CHUNK, LANES), lambda bi, h, n: (h, 0, 0))
    in_specs = [qkv_spec(OFF_RQ), qkv_spec(OFF_RK), qkv_spec(OFF_RV),
                pl.BlockSpec((rows, LANES), lambda bi, h, n: (n, 0)),
                pl.BlockSpec((rows, LANES), lambda bi, h, n: (n, 0)),
                tab_spec, tab_spec, tab_spec, tab_spec]
    args = [z3, z3, z3, cos, sin, dmat, din, dout, dch]
    if has_state:
        in_specs.append(pl.BlockSpec((1, 1, RET_DK, RET_DV), lambda bi, h, n: (bi, h, 0, 0)))
        args.append(state0)
    scratch = [pltpu.VMEM((RET_DK, RET_DV), F32)]
    if rows < RET_CHUNK:
        scratch.append(pltpu.VMEM((RET_CHUNK, LANES), F32))
    return pl.pallas_call(
        functools.partial(_ret_kernel, rows=rows, has_state=has_state),
        grid=(b, RET_HEADS, nchunks),
        in_specs=in_specs,
        out_specs=[pl.BlockSpec((1, rows, RET_DV), lambda bi, h, n: (bi, n, h)),
                   pl.BlockSpec((1, 1, RET_DK, RET_DV), lambda bi, h, n: (bi, h, 0, 0))],
        out_shape=[jax.ShapeDtypeStruct((b, s, RET_W), F32),
                   jax.ShapeDtypeStruct((b, RET_HEADS, RET_DK, RET_DV), F32)],
        scratch_shapes=scratch,
        compiler_params=_cparams("parallel", "parallel", "arbitrary"),
    )(*args)


def _lambda_value(lq1_ref, lk1_ref, lq2_ref, lk2_ref):
    a = jnp.sum(lq1_ref[...] * lk1_ref[...], axis=-1, keepdims=True)
    b = jnp.sum(lq2_ref[...] * lk2_ref[...], axis=-1, keepdims=True)
    return jnp.exp(a) - jnp.exp(b) + LAMBDA_INIT


def _online_update(s, v, m_ref, l_ref, acc_ref, idx):
    m_prev = m_ref[idx]
    m_new = jnp.maximum(m_prev, jnp.max(s, axis=-1, keepdims=True))
    alpha = jnp.exp(m_prev - m_new)
    p = jnp.exp(s - m_new[:, 0:1])
    l_ref[idx] = alpha * l_ref[idx] + jnp.sum(p, axis=-1, keepdims=True)
    acc_ref[idx] = alpha[:, 0:1] * acc_ref[idx] + _dot(p.astype(BF16), v)
    m_ref[idx] = m_new


def _diff_prompt_kernel(q_ref, k_ref, v_ref, lq1, lk1, lq2, lk2, o_ref, m_ref, l_ref, acc_ref, *, tq, tk):
    qi, ki = pl.program_id(2), pl.program_id(3)
    scale = DIFF_DH ** -0.5

    @pl.when(ki == 0)
    def _():
        m_ref[...] = jnp.full_like(m_ref, NEG_INF)
        l_ref[...] = jnp.zeros_like(l_ref)
        acc_ref[...] = jnp.zeros_like(acc_ref)

    def step(masked):
        q = q_ref[0]
        k = k_ref[0]
        v = v_ref[0].astype(BF16)
        for c in range(2):
            sl = slice(c * DIFF_DH, (c + 1) * DIFF_DH)
            s = _dot_nt(q[:, sl].astype(BF16), k[:, sl].astype(BF16)) * scale
            if masked:
                row = lax.broadcasted_iota(jnp.int32, s.shape, 0)
                col = lax.broadcasted_iota(jnp.int32, s.shape, 1)
                s = jnp.where(row >= col, s, NEG_INF)
            _online_update(s, v, m_ref, l_ref, acc_ref, c)

    @pl.when(ki < qi)
    def _():
        step(False)

    @pl.when(ki == qi)
    def _():
        step(True)
        lam = _lambda_value(lq1, lk1, lq2, lk2)
        o0 = acc_ref[0] / l_ref[0][:, 0:1]
        o1 = acc_ref[1] / l_ref[1][:, 0:1]
        o_ref[0] = o0 - lam * o1


def _diff_prompt(z3, lams, *, tq=256):
    b, s, _ = z3.shape
    tk = tq
    nq = s // tq
    col = lambda off: off // DIFF_HW
    lam_spec = pl.BlockSpec((1, DIFF_DH), lambda bi, h, qi, ki: (0, 0))
    return pl.pallas_call(
        functools.partial(_diff_prompt_kernel, tq=tq, tk=tk),
        grid=(b, DIFF_HEADS, nq, nq),
        in_specs=[
            pl.BlockSpec((1, tq, DIFF_HW), lambda bi, h, qi, ki: (bi, qi, col(OFF_DQ) + h)),
            pl.BlockSpec((1, tk, DIFF_HW), lambda bi, h, qi, ki: (bi, jnp.minimum(ki, qi), col(OFF_DK) + h)),
            pl.BlockSpec((1, tk, DIFF_HW), lambda bi, h, qi, ki: (bi, jnp.minimum(ki, qi), col(OFF_DV) + h)),
            lam_spec, lam_spec, lam_spec, lam_spec],
        out_specs=pl.BlockSpec((1, tq, DIFF_HW), lambda bi, h, qi, ki: (bi, qi, h)),
        out_shape=jax.ShapeDtypeStruct((b, s, DIFF_W), F32),
        scratch_shapes=[pltpu.VMEM((2, tq, LANES), F32), pltpu.VMEM((2, tq, LANES), F32),
                        pltpu.VMEM((2, tq, DIFF_HW), F32)],
        compiler_params=_cparams("parallel", "parallel", "parallel", "arbitrary"),
    )(z3, z3, z3, *lams)


def _diff_sample_kernel(pt_ref, q_ref, kn_ref, vn_ref, kp_ref, vp_ref, lq1, lk1, lq2, lk2, o_ref,
                        qbd_ref, m_ref, l_ref, acc_ref, pad_ref, *, t):
    p = pl.program_id(1)
    scale = DIFF_DH ** -0.5
    rows = 2 * t

    @pl.when(p == 0)
    def _():
        m_ref[...] = jnp.full_like(m_ref, NEG_INF)
        l_ref[...] = jnp.zeros_like(l_ref)
        acc_ref[...] = jnp.zeros_like(acc_ref)
        qbd_ref[...] = jnp.zeros_like(qbd_ref)
        q = q_ref[0]
        for h in range(DIFF_HEADS):
            base = h * DIFF_HW
            qbd_ref[h, 0:t, 0:DIFF_DH] = q[:, base:base + DIFF_DH]
            qbd_ref[h, t:rows, DIFF_DH:DIFF_HW] = q[:, base + DIFF_DH:base + DIFF_HW]

    k = kp_ref[0]
    v = vp_ref[0]
    for h in range(DIFF_HEADS):
        sl = slice(h * DIFF_HW, (h + 1) * DIFF_HW)
        s = _dot_nt(qbd_ref[h].astype(BF16), k[:, sl].astype(BF16)) * scale
        _online_update(s, v[:, sl].astype(BF16), m_ref, l_ref, acc_ref, h)

    @pl.when(p == pl.num_programs(1) - 1)
    def _():
        lam = _lambda_value(lq1, lk1, lq2, lk2)
        for h in range(DIFF_HEADS):
            sl = slice(h * DIFF_HW, (h + 1) * DIFF_HW)
            pad_ref[...] = jnp.zeros_like(pad_ref)
            pad_ref[0, 0:t, :] = kn_ref[0][:, sl]
            pad_ref[1, 0:t, :] = vn_ref[0][:, sl]
            s = _dot_nt(qbd_ref[h].astype(BF16), pad_ref[0].astype(BF16)) * scale
            row = lax.broadcasted_iota(jnp.int32, s.shape, 0)
            col = lax.broadcasted_iota(jnp.int32, s.shape, 1)
            tok = jnp.where(row >= t, row - t, row)
            s = jnp.where(col <= tok, s, NEG_INF)
            _online_update(s, pad_ref[1].astype(BF16), m_ref, l_ref, acc_ref, h)
            o = acc_ref[h] / l_ref[h][:, 0:1]
            o_ref[0, :, sl] = o[0:t] - lam * o[t:rows]


def _diff_sample(zs3, cache_k, cache_v, page_table, lams):
    b, t, _ = zs3.shape
    n_pages = page_table.shape[1]
    n_pool = cache_k.shape[0]
    ck = cache_k.reshape(n_pool, PAGE_SIZE, DIFF_W)
    cv = cache_v.reshape(n_pool, PAGE_SIZE, DIFF_W)
    zcol = lambda off: (lambda bi, p, pt: (bi, 0, off // DIFF_W))
    page = lambda bi, p, pt: (pt[bi, p], 0, 0)
    lam_spec = pl.BlockSpec((1, DIFF_DH), lambda bi, p, pt: (0, 0))
    grid_spec = pltpu.PrefetchScalarGridSpec(
        num_scalar_prefetch=1,
        grid=(b, n_pages),
        in_specs=[
            pl.BlockSpec((1, t, DIFF_W), zcol(OFF_DQ)),
            pl.BlockSpec((1, t, DIFF_W), zcol(OFF_DK)),
            pl.BlockSpec((1, t, DIFF_W), zcol(OFF_DV)),
            pl.BlockSpec((1, PAGE_SIZE, DIFF_W), page),
            pl.BlockSpec((1, PAGE_SIZE, DIFF_W), page),
            lam_spec, lam_spec, lam_spec, lam_spec],
        out_specs=pl.BlockSpec((1, t, DIFF_W), lambda bi, p, pt: (bi, 0, 0)),
        scratch_shapes=[pltpu.VMEM((DIFF_HEADS, 2 * t, DIFF_HW), F32),
                        pltpu.VMEM((DIFF_HEADS, 2 * t, LANES), F32),
                        pltpu.VMEM((DIFF_HEADS, 2 * t, LANES), F32),
                        pltpu.VMEM((DIFF_HEADS, 2 * t, DIFF_HW), F32),
                        pltpu.VMEM((2, PAGE_SIZE, DIFF_HW), F32)])
    return pl.pallas_call(
        functools.partial(_diff_sample_kernel, t=t),
        grid_spec=grid_spec,
        out_shape=jax.ShapeDtypeStruct((b, t, DIFF_W), F32),
        compiler_params=_cparams("parallel", "arbitrary"),
    )(page_table, zs3, zs3, zs3, ck, cv, *lams)


def _mix_kernel(ro_ref, rg_ref, do_ref, ga_ref, gb_ref, gr_ref, gd_ref, wa_ref, wb_ref, m_ref, ry_ref, dy_ref):
    @pl.when(pl.program_id(1) == 0)
    def _():
        for h in range(RET_HEADS):
            sl = slice(h * RET_DV, (h + 1) * RET_DV)
            o = ro_ref[:, sl]
            y = o * lax.rsqrt(jnp.mean(o * o, axis=-1, keepdims=True) + EPS) * gr_ref[:, sl]
            g = rg_ref[:, sl]
            ry_ref[:, sl] = (y * (g * jax.nn.sigmoid(g))).astype(BF16)
        for h in range(DIFF_HEADS):
            sl = slice(h * DIFF_HW, (h + 1) * DIFF_HW)
            o = do_ref[:, sl]
            y = o * lax.rsqrt(jnp.mean(o * o, axis=-1, keepdims=True) + EPS) * gd_ref[...] * (1.0 - LAMBDA_INIT)
            dy_ref[:, sl] = y.astype(BF16)

    pa = _dot(ry_ref[...], wa_ref[...])
    pb = _dot(dy_ref[...], wb_ref[...])
    m = jax.nn.sigmoid(ga_ref[...]) * pa + jax.nn.sigmoid(gb_ref[...]) * pb
    m_ref[...] = m.astype(m_ref.dtype)


def _mixer_gate(z2, ret_o, diff_o, ret_norm_g, diff_norm_g, wa, wb, *, bm=512, bn=512):
    m = z2.shape[0]
    bm = min(bm, m)
    return pl.pallas_call(
        _mix_kernel,
        grid=(m // bm, D_MODEL // bn),
        in_specs=[
            pl.BlockSpec((bm, RET_W), lambda i, j: (i, 0)),
            pl.BlockSpec((bm, RET_W), lambda i, j: (i, OFF_RG // RET_W)),
            pl.BlockSpec((bm, DIFF_W), lambda i, j: (i, 0)),
            pl.BlockSpec((bm, bn), lambda i, j: (i, OFF_GA // bn + j)),
            pl.BlockSpec((bm, bn), lambda i, j: (i, OFF_GB // bn + j)),
            pl.BlockSpec((1, RET_W), lambda i, j: (0, 0)),
            pl.BlockSpec((1, DIFF_HW), lambda i, j: (0, 0)),
            pl.BlockSpec((RET_W, bn), lambda i, j: (0, j)),
            pl.BlockSpec((DIFF_W, bn), lambda i, j: (0, j))],
        out_specs=pl.BlockSpec((bm, bn), lambda i, j: (i, j)),
        out_shape=jax.ShapeDtypeStruct((m, D_MODEL), BF16),
        scratch_shapes=[pltpu.VMEM((bm, RET_W), BF16), pltpu.VMEM((bm, DIFF_W), BF16)],
        compiler_params=_cparams("parallel", "arbitrary"),
    )(ret_o, z2, diff_o, z2, z2, ret_norm_g.reshape(1, RET_W), diff_norm_g.reshape(1, DIFF_HW), wa, wb)


def _xattn_kernel(q_ref, k_ref, v_ref, o_ref):
    scale = X_DH ** -0.5
    q = q_ref[0]
    k = k_ref[0]
    v = v_ref[0]
    for h in range(X_HEADS):
        sl = slice(h * X_DH, (h + 1) * X_DH)
        s = _dot_nt(q[:, sl].astype(BF16), k[:, sl].astype(BF16)) * scale
        s = s - jnp.max(s, axis=-1, keepdims=True)
        p = jnp.exp(s)
        p = p / jnp.sum(p, axis=-1, keepdims=True)
        o_ref[0, :, sl] = _dot(p.astype(BF16), v[:, sl].astype(BF16)).astype(o_ref.dtype)


def _cross_attention(q3, mem_k, mem_v, *, tq=512):
    b, s, w = q3.shape
    tq = min(tq, s)
    mk = mem_k.reshape(b, MEM_LEN, w)
    mv = mem_v.reshape(b, MEM_LEN, w)
    return pl.pallas_call(
        _xattn_kernel,
        grid=(b, s // tq),
        in_specs=[pl.BlockSpec((1, tq, w), lambda bi, i: (bi, i, 0)),
                  pl.BlockSpec((1, MEM_LEN, w), lambda bi, i: (bi, 0, 0)),
                  pl.BlockSpec((1, MEM_LEN, w), lambda bi, i: (bi, 0, 0))],
        out_specs=pl.BlockSpec((1, tq, w), lambda bi, i: (bi, i, 0)),
        out_shape=jax.ShapeDtypeStruct((b, s, w), BF16),
        compiler_params=_cparams("parallel", "parallel"),
    )(q3, mk, mv)


def _topk_rows(s, work_ref, rank_ref, sorted_ref):
    nk = s.shape[0]
    iota = lax.broadcasted_iota(jnp.int32, s.shape, 0).astype(F32)
    work_ref[...] = s
    rank_ref[...] = jnp.full(s.shape, float(PEER_TOPK), F32)
    for r in range(PEER_TOPK):
        w = work_ref[...]
        m = jnp.max(w, axis=0, keepdims=True)
        idx = jnp.min(jnp.where(w == m, iota, float(nk)), axis=0, keepdims=True)
        sel = iota == idx
        rank_ref[...] = jnp.where(sel, float(r), rank_ref[...])
        work_ref[...] = jnp.where(sel, NEG_INF, w)
        sorted_ref[r:r + 1, :] = m


def _route_kernel(q_ref, keys_ref, e1_ref, n1_ref, e2_ref, r2_ref,
                  work_ref, rank1_ref, rank2_ref, s1s_ref, s2s_ref):
    q = q_ref[...]
    st = []
    for c in range(2):
        qc = q[:, c * PEER_HALF:(c + 1) * PEER_HALF].astype(BF16)
        st.append(_dot_nt(keys_ref[c, 0].astype(BF16), qc))
    _topk_rows(st[0], work_ref, rank1_ref, s1s_ref)
    _topk_rows(st[1], work_ref, rank2_ref, s2s_ref)
    s1 = s1s_ref[...]
    s2 = s2s_ref[...]

    iota = lax.broadcasted_iota(jnp.int32, s1.shape, 0).astype(F32)
    ptr = jnp.zeros(s1.shape, F32)
    count = jnp.zeros(s1.shape, F32)
    front = s1 + s2[0:1, :]
    zsum = jnp.zeros((1, s1.shape[1]), F32)
    top0 = None
    for kk in range(PEER_TOPK):
        m = jnp.max(front, axis=0, keepdims=True)
        if kk == 0:
            top0 = m
        zsum = zsum + jnp.exp(m - top0)
        istar = jnp.min(jnp.where(front == m, iota, float(PEER_TOPK)), axis=0, keepdims=True)
        oh = iota == istar
        count = count + jnp.where(oh, 1.0, 0.0)
        pnew = jnp.sum(jnp.where(oh, ptr, 0.0), axis=0, keepdims=True) + 1.0
        ptr = jnp.where(oh, pnew, ptr)
        s2n = jnp.max(jnp.where(iota == pnew, s2, NEG_INF), axis=0, keepdims=True)
        s1sel = jnp.max(jnp.where(oh, s1, NEG_INF), axis=0, keepdims=True)
        front = jnp.where(oh, s1sel + s2n, front)

    rank1 = rank1_ref[...]
    rank2 = rank2_ref[...]
    n1 = jnp.zeros(rank1.shape, F32)
    for r in range(PEER_TOPK):
        n1 = jnp.where(rank1 == float(r), count[r:r + 1, :], n1)
    inv_z = 1.0 / zsum
    e1_ref[0] = jnp.where(rank1 < PEER_TOPK, jnp.exp(st[0] - s1[0:1, :]) * inv_z, 0.0)
    e2_ref[0] = jnp.where(rank2 < PEER_TOPK, jnp.exp(st[1] - s2[0:1, :]), 0.0)
    n1_ref[0] = n1
    r2_ref[0] = rank2


def _peer_route(qp, peer_keys, *, tt=256):
    t = qp.shape[0]
    tt = min(tt, t)
    hw = 2 * PEER_HALF
    out = jax.ShapeDtypeStruct((PEER_HEADS, PEER_NKEYS, t), F32)
    ospec = pl.BlockSpec((1, PEER_NKEYS, tt), lambda i, h: (h, 0, i))
    return pl.pallas_call(
        _route_kernel,
        grid=(t // tt, PEER_HEADS),
        in_specs=[pl.BlockSpec((tt, hw), lambda i, h: (i, h)),
                  pl.BlockSpec((2, 1, PEER_NKEYS, PEER_HALF), lambda i, h: (0, h, 0, 0))],
        out_specs=[ospec, ospec, ospec, ospec],
        out_shape=[out, out, out, out],
        scratch_shapes=[pltpu.VMEM((PEER_NKEYS, tt), F32), pltpu.VMEM((PEER_NKEYS, tt), F32),
                        pltpu.VMEM((PEER_NKEYS, tt), F32),
                        pltpu.VMEM((PEER_TOPK, tt), F32), pltpu.VMEM((PEER_TOPK, tt), F32)],
        compiler_params=_cparams("parallel", "parallel"),
    )(qp, peer_keys)


def _peer_dense_kernel(h2_ref, gf_ref, gl_ref, u_ref, v_ref, e1_ref, n1_ref, e2_ref, r2_ref, y_ref,
                       xn_ref, acc_ref, *, et):
    e = pl.program_id(1)

    @pl.when(e == 0)
    def _():
        x = h2_ref[...]
        ms = jnp.mean(x * x, axis=-1, keepdims=True)
        xn_ref[...] = (x * lax.rsqrt(ms + EPS) * gf_ref[...]).astype(BF16)
        acc_ref[...] = jnp.zeros_like(acc_ref)

    ht = _dot_nt(u_ref[...], xn_ref[...])
    act = jax.nn.gelu(ht)
    slabs = et // PEER_NKEYS
    ws = []
    for s in range(slabs):
        a = e * slabs + s
        w = jnp.zeros((PEER_NKEYS, ht.shape[1]), F32)
        for h in range(PEER_HEADS):
            e1row = e1_ref[h, pl.ds(a, 1), :]
            n1row = n1_ref[h, pl.ds(a, 1), :]
            w = w + jnp.where(r2_ref[h] < n1row, e2_ref[h], 0.0) * e1row
        ws.append(w)
    wt = ws[0] if slabs == 1 else jnp.concatenate(ws, axis=0)
    yt = (act * wt).astype(BF16)
    acc_ref[...] += _dot_tn(yt, v_ref[...])

    @pl.when(e == pl.num_programs(1) - 1)
    def _():
        r = h2_ref[...] + acc_ref[...]
        ms = jnp.mean(r * r, axis=-1, keepdims=True)
        y_ref[...] = r * lax.rsqrt(ms + EPS) * gl_ref[...]


def _peer_dense(h2, g_ffn, g_final, u, v, route, *, tt=512, et=512):
    t = h2.shape[0]
    tt = min(tt, t)
    n_exp = u.shape[0]
    rspec = pl.BlockSpec((PEER_HEADS, PEER_NKEYS, tt), lambda i, e: (0, 0, i))
    return pl.pallas_call(
        functools.partial(_peer_dense_kernel, et=et),
        grid=(t // tt, n_exp // et),
        in_specs=[pl.BlockSpec((tt, D_MODEL), lambda i, e: (i, 0)),
                  pl.BlockSpec((1, D_MODEL), lambda i, e: (0, 0)),
                  pl.BlockSpec((1, D_MODEL), lambda i, e: (0, 0)),
                  pl.BlockSpec((et, D_MODEL), lambda i, e: (e, 0)),
                  pl.BlockSpec((et, D_MODEL), lambda i, e: (e, 0)),
                  rspec, rspec, rspec, rspec],
        out_specs=pl.BlockSpec((tt, D_MODEL), lambda i, e: (i, 0)),
        out_shape=jax.ShapeDtypeStruct((t, D_MODEL), F32),
        scratch_shapes=[pltpu.VMEM((tt, D_MODEL), BF16), pltpu.VMEM((tt, D_MODEL), F32)],
        compiler_params=_cparams("parallel", "arbitrary", vmem=PEER_VMEM_LIMIT),
    )(h2, g_ffn.reshape(1, D_MODEL), g_final.reshape(1, D_MODEL), u, v, *route)


def _tail(h1, mem_k, mem_v, w, batch):
    t = h1.shape[0]
    qx = _matmul(h1, w["xq"], gain=w["g_cross"])
    ox = _cross_attention(qx.reshape(batch, t // batch, X_HEADS * X_DH), mem_k, mem_v)
    h2 = _matmul(ox.reshape(t, X_HEADS * X_DH), w["xo"], residual=h1)
    qp = _matmul(h2, w["pq"], gain=w["g_ffn"])
    route = _peer_route(qp, w["peer_keys"])
    return _peer_dense(h2, w["g_ffn"], w["g_final"], w["peer_u"], w["peer_v"], route)


def _mixer_out(x2, z2, ret_o, diff_o, w):
    m = _mixer_gate(z2, ret_o, diff_o, w["ret_norm_g"], w["diff_norm_g"], w["branch_a"], w["branch_b"])
    return _matmul(m, w["out"], residual=x2)


def kernel(x_prompt, x_sample, cache_k, cache_v, state_ret, cache_mem_k, cache_mem_v, page_table, mem_prompt, g_mix, w_in, ret_norm_g, diff_norm_g, lambda_q1, lambda_k1, lambda_q2, lambda_k2, w_branch_a, w_branch_b, w_out, g_cross, w_xq, w_mem_kv, w_xo, g_ffn, w_pq, peer_keys, peer_u, peer_v, g_final):
    bp, sp, d = x_prompt.shape
    bd, td, _ = x_sample.shape
    w = dict(
        ret_norm_g=ret_norm_g, diff_norm_g=diff_norm_g, g_cross=g_cross, g_ffn=g_ffn, g_final=g_final,
        branch_a=w_branch_a.astype(BF16), branch_b=w_branch_b.astype(BF16), out=w_out.astype(BF16),
        xq=w_xq.astype(BF16), xo=w_xo.astype(BF16), pq=w_pq.astype(BF16), peer_keys=peer_keys,
        peer_u=peer_u.astype(BF16), peer_v=peer_v.astype(BF16))
    w_in_b = w_in.astype(BF16)
    lams = [a.reshape(1, DIFF_DH).astype(F32) for a in (lambda_q1, lambda_k1, lambda_q2, lambda_k2)]

    xp2 = x_prompt.reshape(bp * sp, d)
    z2 = _matmul(xp2, w_in_b, gain=g_mix)
    z3 = z2.reshape(bp, sp, IN_W)
    ret_o, ret_state_prompt = _retention(z3, jnp.arange(sp, dtype=jnp.int32), None, RET_CHUNK)
    diff_o = _diff_prompt(z3, lams)
    h1 = _mixer_out(xp2, z2, ret_o.reshape(bp * sp, RET_W), diff_o.reshape(bp * sp, DIFF_W), w)
    mem_kv = _matmul(mem_prompt.reshape(bp * MEM_LEN, d), w_mem_kv.astype(BF16))
    mem_kv = mem_kv.reshape(bp, MEM_LEN, 2, X_HEADS, X_DH)
    mem_k_prompt, mem_v_prompt = mem_kv[:, :, 0], mem_kv[:, :, 1]
    y_prompt = _tail(h1, mem_k_prompt, mem_v_prompt, w, bp).reshape(bp, sp, d)
    k_prompt = z3[:, :, OFF_DK:OFF_DK + DIFF_W].reshape(bp, sp, DIFF_HEADS, DIFF_HW)
    v_prompt = z3[:, :, OFF_DV:OFF_DV + DIFF_W].reshape(bp, sp, DIFF_HEADS, DIFF_HW)

    past = page_table.shape[1] * PAGE_SIZE
    xs2 = x_sample.reshape(bd * td, d)
    zs2 = _matmul(xs2, w_in_b, gain=g_mix)
    zs3 = zs2.reshape(bd, td, IN_W)
    ret_o_s, ret_state_sample = _retention(zs3, past + jnp.arange(td, dtype=jnp.int32), state_ret, td)
    diff_o_s = _diff_sample(zs3, cache_k, cache_v, page_table, lams)
    h1s = _mixer_out(xs2, zs2, ret_o_s.reshape(bd * td, RET_W), diff_o_s.reshape(bd * td, DIFF_W), w)
    y_sample = _tail(h1s, cache_mem_k, cache_mem_v, w, bd).reshape(bd, td, d)
    k_sample = zs3[:, :, OFF_DK:OFF_DK + DIFF_W].reshape(bd, td, DIFF_HEADS, DIFF_HW)
    v_sample = zs3[:, :, OFF_DV:OFF_DV + DIFF_W].reshape(bd, td, DIFF_HEADS, DIFF_HW)

    return (y_prompt, y_sample, k_prompt, v_prompt, ret_state_prompt, mem_k_prompt, mem_v_prompt,
            k_sample, v_sample, ret_state_sample)
```

```python
import functools
import math

import numpy as np
import jax
import jax.numpy as jnp
from jax import lax
from jax.experimental import pallas as pl
from jax.experimental.pallas import tpu as pltpu

F32 = jnp.float32
BF16 = jnp.bfloat16

D_MODEL = 2048
RET_HEADS = 8
RET_DK = 128
RET_DV = 128
RET_CHUNK = 128
DIFF_HEADS = 4
DIFF_DH = 128
DIFF_HW = 2 * DIFF_DH
PAGE_SIZE = 128
MEM_LEN = 256
X_HEADS = 4
X_DH = 128
PEER_HEADS = 8
PEER_NKEYS = 128
PEER_HALF = 128
PEER_TOPK = 16
ROPE_BASE = 10000.0
LAMBDA_INIT = 0.8 - 0.6 * math.exp(-0.3 * 0)
EPS = 1e-6

RET_W = RET_HEADS * RET_DK
DIFF_W = DIFF_HEADS * DIFF_HW
OFF_RQ, OFF_RK, OFF_RV, OFF_RG = 0, RET_W, 2 * RET_W, 3 * RET_W
OFF_DQ = 4 * RET_W
OFF_DK = OFF_DQ + DIFF_W
OFF_DV = OFF_DK + DIFF_W
OFF_GA = OFF_DV + DIFF_W
OFF_GB = OFF_GA + D_MODEL
IN_W = OFF_GB + D_MODEL

LANES = 128
VMEM_LIMIT = 48 * 1024 * 1024
PEER_VMEM_LIMIT = 58 * 1024 * 1024
SAMPLE_PAGES_PER_STEP = 8

NEG_INF = float("-inf")


def _cparams(*sem, vmem=VMEM_LIMIT, flags=None):
    return pltpu.CompilerParams(dimension_semantics=sem, vmem_limit_bytes=vmem, flags=flags)


def _dot(a, b):
    return jnp.dot(a, b, preferred_element_type=F32)


def _dot_nt(a, b):
    return lax.dot_general(a, b, (((1,), (1,)), ((), ())), preferred_element_type=F32)


def _dot_tn(a, b):
    return lax.dot_general(a, b, (((0,), (0,)), ((), ())), preferred_element_type=F32)


def _mm_kernel(*refs, norm, residual):
    it = iter(refs)
    x_ref = next(it)
    g_ref = next(it) if norm else None
    w_ref = next(it)
    r_ref = next(it) if residual else None
    o_ref = next(it)
    xs_ref = next(it)

    @pl.when(pl.program_id(1) == 0)
    def _():
        x = x_ref[...].astype(F32)
        if norm:
            ms = jnp.mean(x * x, axis=-1, keepdims=True)
            x = x * lax.rsqrt(ms + EPS) * g_ref[...]
        xs_ref[...] = x.astype(BF16)

    acc = _dot(xs_ref[...], w_ref[...])
    if residual:
        acc = r_ref[...] + acc
    o_ref[...] = acc.astype(o_ref.dtype)


def _matmul(x, w, *, name, gain=None, residual=None, out_dtype=F32, bm=1024, bn=512):
    m, k = x.shape
    n = w.shape[1]
    bm = min(bm, m)
    bn = min(bn, n)
    assert m % bm == 0 and n % bn == 0
    norm = gain is not None
    res = residual is not None
    in_specs = [pl.BlockSpec((bm, k), lambda i, j: (i, 0))]
    args = [x]
    if norm:
        in_specs.append(pl.BlockSpec((1, k), lambda i, j: (0, 0)))
        args.append(gain.reshape(1, k).astype(F32))
    in_specs.append(pl.BlockSpec((k, bn), lambda i, j: (0, j)))
    args.append(w)
    if res:
        in_specs.append(pl.BlockSpec((bm, bn), lambda i, j: (i, j)))
        args.append(residual)
    return pl.pallas_call(
        functools.partial(_mm_kernel, norm=norm, residual=res),
        grid=(m // bm, n // bn),
        in_specs=in_specs,
        out_specs=pl.BlockSpec((bm, bn), lambda i, j: (i, j)),
        out_shape=jax.ShapeDtypeStruct((m, n), out_dtype),
        scratch_shapes=[pltpu.VMEM((bm, k), BF16)],
        compiler_params=_cparams("parallel", "arbitrary"),
        name=name,
    )(*args)


def _ret_tables(chunk_len):
    c = RET_CHUNK
    lg = jnp.log1p(-jnp.exp2(-5.0 - jnp.arange(RET_HEADS, dtype=F32)))
    i = jnp.arange(c, dtype=F32)
    rel = i[:, None] - i[None, :]
    causal = rel >= 0
    dmat = jnp.where(causal[None], jnp.exp(jnp.where(causal, rel, 0.0)[None] * lg[:, None, None]), 0.0)
    dec_in = jnp.exp((i + 1.0)[None, :] * lg[:, None])
    dec_out = jnp.exp((chunk_len - 1.0 - i)[None, :] * lg[:, None])
    dec_chunk = jnp.exp(chunk_len * lg)
    ones = jnp.ones((RET_HEADS, c, LANES), F32)
    return (dmat, dec_in[:, :, None] * ones, dec_out[:, :, None] * ones,
            dec_chunk[:, None, None] * ones)


def _rope_tables(pos):
    half = RET_DK // 2
    freqs = jnp.exp(-math.log(ROPE_BASE) * jnp.arange(half, dtype=F32) / half)
    ang = pos.astype(F32)[:, None] * freqs[None, :]
    cos, sin = jnp.cos(ang), jnp.sin(ang)
    return jnp.concatenate([cos, cos], axis=-1), jnp.concatenate([-sin, sin], axis=-1)


def _ret_kernel(*refs, rows, has_state):
    it = iter(refs)
    q_ref, k_ref, v_ref, cos_ref, sin_ref = (next(it) for _ in range(5))
    dmat_ref, din_ref, dout_ref, dch_ref = (next(it) for _ in range(4))
    s0_ref = next(it) if has_state else None
    o_ref, sfin_ref, s_ref = next(it), next(it), next(it)
    pad_ref = next(it) if rows < RET_CHUNK else None
    n = pl.program_id(1)

    @pl.when(n == 0)
    def _():
        if has_state:
            s_ref[...] = s0_ref[0]
        else:
            s_ref[...] = jnp.zeros_like(s_ref)

    def full(x):
        if rows == RET_CHUNK:
            return x
        width = x.shape[1]
        pad_ref[:, 0:width] = jnp.zeros((RET_CHUNK, width), F32)
        pad_ref[0:rows, 0:width] = x
        return pad_ref[:, 0:width]

    def rot(x, cos, sin):
        return x * cos + pltpu.roll(x, RET_DK // 2, axis=1) * sin

    cos, sin = full(cos_ref[...]), full(sin_ref[...])
    q_all, k_all, v_all = full(q_ref[0]), full(k_ref[0]), full(v_ref[0])
    for h in range(RET_HEADS):
        sl = slice(h * RET_DK, (h + 1) * RET_DK)
        q = rot(q_all[:, sl], cos, sin)
        k = rot(k_all[:, sl], cos, sin) * (RET_DK ** -0.5)
        v = v_all[:, sl].astype(BF16)
        s_prev = s_ref[h]
        qb = q.astype(BF16)
        scores = _dot_nt(qb, k.astype(BF16)) * dmat_ref[h]
        out = _dot(scores.astype(BF16), v) + _dot(qb, s_prev.astype(BF16)) * din_ref[h]
        s_ref[h] = dch_ref[h] * s_prev + _dot_tn((k * dout_ref[h]).astype(BF16), v)
        o_ref[0, :, sl] = out if rows == RET_CHUNK else out[0:rows, :]

    @pl.when(n == pl.num_programs(1) - 1)
    def _():
        sfin_ref[0] = s_ref[...]


def _retention(z3, pos, state0, chunk_len, name):
    b, s, _ = z3.shape
    rows = min(s, RET_CHUNK)
    nchunks = s // rows
    cos, sin = _rope_tables(pos)
    dmat, din, dout, dch = _ret_tables(float(chunk_len))
    has_state = state0 is not None
    qkv_spec = lambda off: pl.BlockSpec((1, rows, RET_W), lambda bi, n: (bi, n, off // RET_W))
    tab_spec = pl.BlockSpec((RET_HEADS, RET_CHUNK, LANES), lambda bi, n: (0, 0, 0))
    state_spec = pl.BlockSpec((1, RET_HEADS, RET_DK, RET_DV), lambda bi, n: (bi, 0, 0, 0))
    in_specs = [qkv_spec(OFF_RQ), qkv_spec(OFF_RK), qkv_spec(OFF_RV),
                pl.BlockSpec((rows, LANES), lambda bi, n: (n, 0)),
                pl.BlockSpec((rows, LANES), lambda bi, n: (n, 0)),
                tab_spec, tab_spec, tab_spec, tab_spec]
    args = [z3, z3, z3, cos, sin, dmat, din, dout, dch]
    if has_state:
        in_specs.append(state_spec)
        args.append(state0)
    scratch = [pltpu.VMEM((RET_HEADS, RET_DK, RET_DV), F32)]
    if rows < RET_CHUNK:
        scratch.append(pltpu.VMEM((RET_CHUNK, RET_W), F32))
    return pl.pallas_call(
        functools.partial(_ret_kernel, rows=rows, has_state=has_state),
        grid=(b, nchunks),
        in_specs=in_specs,
        out_specs=[pl.BlockSpec((1, rows, RET_W), lambda bi, n: (bi, n, 0)), state_spec],
        out_shape=[jax.ShapeDtypeStruct((b, s, RET_W), F32),
                   jax.ShapeDtypeStruct((b, RET_HEADS, RET_DK, RET_DV), F32)],
        scratch_shapes=scratch,
        compiler_params=_cparams("parallel", "arbitrary"),
        name=name,
    )(*args)


def _lambda_value(lq1_ref, lk1_ref, lq2_ref, lk2_ref):
    a = jnp.sum(lq1_ref[...] * lk1_ref[...], axis=-1, keepdims=True)
    b = jnp.sum(lq2_ref[...] * lk2_ref[...], axis=-1, keepdims=True)
    return jnp.exp(a) - jnp.exp(b) + LAMBDA_INIT


def _online_update(s, pv, m_ref, l_ref, acc_ref, idx):
    m_prev = m_ref[idx]
    m_new = jnp.maximum(m_prev, jnp.max(s, axis=-1, keepdims=True))
    alpha = jnp.exp(m_prev - m_new)
    p = jnp.exp(s - m_new[:, 0:1])
    l_ref[idx] = alpha * l_ref[idx] + jnp.sum(p, axis=-1, keepdims=True)
    acc_ref[idx] = alpha[:, 0:1] * acc_ref[idx] + pv(p)
    m_ref[idx] = m_new


def _diff_prompt_kernel(qi_ref, ki_ref, q_ref, k_ref, v_ref, lq1, lk1, lq2, lk2, o_ref,
                        m_ref, l_ref, acc_ref):
    pair = pl.program_id(2)
    qi, ki = qi_ref[pair], ki_ref[pair]
    scale = DIFF_DH ** -0.5

    @pl.when(ki == 0)
    def _():
        m_ref[...] = jnp.full_like(m_ref, NEG_INF)
        l_ref[...] = jnp.zeros_like(l_ref)
        acc_ref[...] = jnp.zeros_like(acc_ref)

    def step(masked):
        q = q_ref[0]
        k = k_ref[0]
        v = v_ref[0].astype(BF16)
        for c in range(2):
            sl = slice(c * DIFF_DH, (c + 1) * DIFF_DH)
            s = _dot_nt(q[:, sl].astype(BF16), k[:, sl].astype(BF16)) * scale
            if masked:
                row = lax.broadcasted_iota(jnp.int32, s.shape, 0)
                col = lax.broadcasted_iota(jnp.int32, s.shape, 1)
                s = jnp.where(row >= col, s, NEG_INF)
            _online_update(s, lambda p: _dot(p.astype(BF16), v), m_ref, l_ref, acc_ref, c)

    @pl.when(ki < qi)
    def _():
        step(False)

    @pl.when(ki == qi)
    def _():
        step(True)
        lam = _lambda_value(lq1, lk1, lq2, lk2)
        o0 = acc_ref[0] / l_ref[0][:, 0:1]
        o1 = acc_ref[1] / l_ref[1][:, 0:1]
        o_ref[0] = o0 - lam * o1


def _diff_prompt(z3, lams, *, tq=512):
    b, s, _ = z3.shape
    nq = s // tq
    pairs = [(qi, ki) for qi in range(nq) for ki in range(qi + 1)]
    qi_tab = jnp.asarray([p[0] for p in pairs], jnp.int32)
    ki_tab = jnp.asarray([p[1] for p in pairs], jnp.int32)
    col = lambda off: off // DIFF_HW
    lam_spec = pl.BlockSpec((1, DIFF_DH), lambda bi, h, p, qt, kt: (0, 0))
    grid_spec = pltpu.PrefetchScalarGridSpec(
        num_scalar_prefetch=2,
        grid=(b, DIFF_HEADS, len(pairs)),
        in_specs=[
            pl.BlockSpec((1, tq, DIFF_HW), lambda bi, h, p, qt, kt: (bi, qt[p], col(OFF_DQ) + h)),
            pl.BlockSpec((1, tq, DIFF_HW), lambda bi, h, p, qt, kt: (bi, kt[p], col(OFF_DK) + h)),
            pl.BlockSpec((1, tq, DIFF_HW), lambda bi, h, p, qt, kt: (bi, kt[p], col(OFF_DV) + h)),
            lam_spec, lam_spec, lam_spec, lam_spec],
        out_specs=pl.BlockSpec((1, tq, DIFF_HW), lambda bi, h, p, qt, kt: (bi, qt[p], h)),
        scratch_shapes=[pltpu.VMEM((2, tq, LANES), F32), pltpu.VMEM((2, tq, LANES), F32),
                        pltpu.VMEM((2, tq, DIFF_HW), F32)])
    return pl.pallas_call(
        _diff_prompt_kernel,
        grid_spec=grid_spec,
        out_shape=jax.ShapeDtypeStruct((b, s, DIFF_W), F32),
        compiler_params=_cparams("parallel", "parallel", "arbitrary"),
        name="diff_prompt",
    )(qi_tab, ki_tab, z3, z3, z3, *lams)


def _diff_sample_kernel(pt_ref, q_ref, kn_ref, vn_ref, *rest, t, g):
    k_refs, v_refs = rest[:g], rest[g:2 * g]
    lq1, lk1, lq2, lk2, o_ref, qall_ref, m_ref, l_ref, acc_ref, pad_ref = rest[2 * g:]
    p = pl.program_id(1)
    scale = DIFF_DH ** -0.5
    hrows = 2 * t
    nrow = DIFF_HEADS * hrows

    @pl.when(p == 0)
    def _():
        m_ref[...] = jnp.full_like(m_ref, NEG_INF)
        l_ref[...] = jnp.zeros_like(l_ref)
        acc_ref[...] = jnp.zeros_like(acc_ref)
        qall_ref[...] = jnp.zeros_like(qall_ref)
        q = q_ref[0]
        for h in range(DIFF_HEADS):
            for c in range(2):
                r0 = h * hrows + c * t
                lo = h * DIFF_HW + c * DIFF_DH
                qall_ref[r0:r0 + t, c * DIFF_DH:(c + 1) * DIFF_DH] = q[:, lo:lo + DIFF_DH]

    qall = qall_ref[...].astype(BF16)

    def gather_heads(ref):
        return jnp.concatenate([ref[0, :, h, :] for h in range(DIFF_HEADS)], axis=0)

    def page_scores(kc):
        st = _dot_nt(kc.astype(BF16), qall) * scale
        lane = lax.broadcasted_iota(jnp.int32, (PAGE_SIZE, LANES), 1)
        sc = st[0:PAGE_SIZE]
        for h in range(1, DIFF_HEADS):
            sc = jnp.where(lane >= h * hrows, st[h * PAGE_SIZE:(h + 1) * PAGE_SIZE], sc)
        return sc.T[0:nrow]

    def update(scores, values):
        row = lax.broadcasted_iota(jnp.int32, (nrow, PAGE_SIZE), 0)
        row_head = jnp.zeros((nrow, PAGE_SIZE), jnp.int32)
        for h in range(1, DIFF_HEADS):
            row_head = jnp.where(row >= h * hrows, h, row_head)

        def pv(pr):
            out = None
            for i, vc in enumerate(values):
                pg = pr[:, i * PAGE_SIZE:(i + 1) * PAGE_SIZE]
                pbd = jnp.concatenate([jnp.where(row_head == h, pg, 0.0) for h in range(DIFF_HEADS)], axis=1)
                o = _dot(pbd.astype(BF16), vc.astype(BF16))
                out = o if out is None else out + o
            return out

        s = scores[0] if len(scores) == 1 else jnp.concatenate(scores, axis=1)
        _online_update(s, pv, m_ref, l_ref, acc_ref, slice(None))

    update([page_scores(gather_heads(k)) for k in k_refs], [gather_heads(v) for v in v_refs])

    @pl.when(p == pl.num_programs(1) - 1)
    def _():
        pad_ref[...] = jnp.zeros_like(pad_ref)
        for h in range(DIFF_HEADS):
            sl = slice(h * DIFF_HW, (h + 1) * DIFF_HW)
            pad_ref[0, h * PAGE_SIZE:h * PAGE_SIZE + t, :] = kn_ref[0][:, sl]
            pad_ref[1, h * PAGE_SIZE:h * PAGE_SIZE + t, :] = vn_ref[0][:, sl]
        s = page_scores(pad_ref[0])
        row = lax.broadcasted_iota(jnp.int32, s.shape, 0)
        col = lax.broadcasted_iota(jnp.int32, s.shape, 1)
        s = jnp.where(col <= jnp.bitwise_and(row, t - 1), s, NEG_INF)
        update([s], [pad_ref[1]])
        lam = _lambda_value(lq1, lk1, lq2, lk2)
        o = acc_ref[...] / l_ref[...][:, 0:1]
        for h in range(DIFF_HEADS):
            r0 = h * hrows
            o_ref[0, :, h * DIFF_HW:(h + 1) * DIFF_HW] = o[r0:r0 + t] - lam * o[r0 + t:r0 + hrows]


def _diff_sample(zs3, cache_k, cache_v, page_table, lams):
    b, t, _ = zs3.shape
    assert t & (t - 1) == 0
    n_pages = page_table.shape[1]
    g = math.gcd(SAMPLE_PAGES_PER_STEP, n_pages)
    nrow = DIFF_HEADS * 2 * t
    zcol = lambda off: (lambda bi, p, pt: (bi, 0, off // DIFF_W))
    page = lambda i: (lambda bi, p, pt: (pt[bi, p * g + i], 0, 0, 0))
    page_spec = lambda i: pl.BlockSpec((1, PAGE_SIZE, DIFF_HEADS, DIFF_HW), page(i))
    lam_spec = pl.BlockSpec((1, DIFF_DH), lambda bi, p, pt: (0, 0))
    grid_spec = pltpu.PrefetchScalarGridSpec(
        num_scalar_prefetch=1,
        grid=(b, n_pages // g),
        in_specs=[
            pl.BlockSpec((1, t, DIFF_W), zcol(OFF_DQ)),
            pl.BlockSpec((1, t, DIFF_W), zcol(OFF_DK)),
            pl.BlockSpec((1, t, DIFF_W), zcol(OFF_DV)),
            *[page_spec(i) for i in range(g)],
            *[page_spec(i) for i in range(g)],
            lam_spec, lam_spec, lam_spec, lam_spec],
        out_specs=pl.BlockSpec((1, t, DIFF_W), lambda bi, p, pt: (bi, 0, 0)),
        scratch_shapes=[pltpu.VMEM((LANES, DIFF_HW), F32),
                        pltpu.VMEM((nrow, LANES), F32),
                        pltpu.VMEM((nrow, LANES), F32),
                        pltpu.VMEM((nrow, DIFF_HW), F32),
                        pltpu.VMEM((2, DIFF_HEADS * PAGE_SIZE, DIFF_HW), F32)])
    return pl.pallas_call(
        functools.partial(_diff_sample_kernel, t=t, g=g),
        grid_spec=grid_spec,
        out_shape=jax.ShapeDtypeStruct((b, t, DIFF_W), F32),
        compiler_params=_cparams("parallel", "arbitrary"),
        name="diff_sample",
    )(page_table, zs3, zs3, zs3, *([cache_k] * g), *([cache_v] * g), *lams)


def _mix_kernel(ro_ref, rg_ref, do_ref, ga_ref, gb_ref, gr_ref, gd_ref, wa_ref, wb_ref, m_ref, ry_ref, dy_ref):
    @pl.when(pl.program_id(1) == 0)
    def _():
        for h in range(RET_HEADS):
            sl = slice(h * RET_DV, (h + 1) * RET_DV)
            o = ro_ref[:, sl]
            y = o * lax.rsqrt(jnp.mean(o * o, axis=-1, keepdims=True) + EPS) * gr_ref[:, sl]
            g = rg_ref[:, sl]
            ry_ref[:, sl] = (y * (g * jax.nn.sigmoid(g))).astype(BF16)
        for h in range(DIFF_HEADS):
            sl = slice(h * DIFF_HW, (h + 1) * DIFF_HW)
            o = do_ref[:, sl]
            y = o * lax.rsqrt(jnp.mean(o * o, axis=-1, keepdims=True) + EPS) * gd_ref[...] * (1.0 - LAMBDA_INIT)
            dy_ref[:, sl] = y.astype(BF16)

    pa = _dot(ry_ref[...], wa_ref[...])
    pb = _dot(dy_ref[...], wb_ref[...])
    m = jax.nn.sigmoid(ga_ref[...]) * pa + jax.nn.sigmoid(gb_ref[...]) * pb
    m_ref[...] = m.astype(m_ref.dtype)


def _mixer_gate(z2, ret_o, diff_o, ret_norm_g, diff_norm_g, wa, wb, *, bm=512, bn=512):
    m = z2.shape[0]
    bm = min(bm, m)
    return pl.pallas_call(
        _mix_kernel,
        grid=(m // bm, D_MODEL // bn),
        in_specs=[
            pl.BlockSpec((bm, RET_W), lambda i, j: (i, 0)),
            pl.BlockSpec((bm, RET_W), lambda i, j: (i, OFF_RG // RET_W)),
            pl.BlockSpec((bm, DIFF_W), lambda i, j: (i, 0)),
            pl.BlockSpec((bm, bn), lambda i, j: (i, OFF_GA // bn + j)),
            pl.BlockSpec((bm, bn), lambda i, j: (i, OFF_GB // bn + j)),
            pl.BlockSpec((1, RET_W), lambda i, j: (0, 0)),
            pl.BlockSpec((1, DIFF_HW), lambda i, j: (0, 0)),
            pl.BlockSpec((RET_W, bn), lambda i, j: (0, j)),
            pl.BlockSpec((DIFF_W, bn), lambda i, j: (0, j))],
        out_specs=pl.BlockSpec((bm, bn), lambda i, j: (i, j)),
        out_shape=jax.ShapeDtypeStruct((m, D_MODEL), BF16),
        scratch_shapes=[pltpu.VMEM((bm, RET_W), BF16), pltpu.VMEM((bm, DIFF_W), BF16)],
        compiler_params=_cparams("parallel", "arbitrary"),
        name="mixer_gate",
    )(ret_o, z2, diff_o, z2, z2, ret_norm_g.reshape(1, RET_W), diff_norm_g.reshape(1, DIFF_HW), wa, wb)


def _xattn_kernel(q_ref, k_ref, v_ref, o_ref):
    scale = X_DH ** -0.5
    q = q_ref[0]
    k = k_ref[0]
    v = v_ref[0]
    for h in range(X_HEADS):
        sl = slice(h * X_DH, (h + 1) * X_DH)
        s = _dot_nt(q[:, sl].astype(BF16), k[:, sl].astype(BF16)) * scale
        s = s - jnp.max(s, axis=-1, keepdims=True)
        p = jnp.exp(s)
        p = p / jnp.sum(p, axis=-1, keepdims=True)
        o_ref[0, :, sl] = _dot(p.astype(BF16), v[:, sl].astype(BF16)).astype(o_ref.dtype)


def _cross_attention(q3, mem_k, mem_v, *, tq=512):
    b, s, w = q3.shape
    tq = min(tq, s)
    mk = mem_k.reshape(b, MEM_LEN, w)
    mv = mem_v.reshape(b, MEM_LEN, w)
    return pl.pallas_call(
        _xattn_kernel,
        grid=(b, s // tq),
        in_specs=[pl.BlockSpec((1, tq, w), lambda bi, i: (bi, i, 0)),
                  pl.BlockSpec((1, MEM_LEN, w), lambda bi, i: (bi, 0, 0)),
                  pl.BlockSpec((1, MEM_LEN, w), lambda bi, i: (bi, 0, 0))],
        out_specs=pl.BlockSpec((1, tq, w), lambda bi, i: (bi, i, 0)),
        out_shape=jax.ShapeDtypeStruct((b, s, w), BF16),
        compiler_params=_cparams("parallel", "parallel"),
        name="cross_attention",
    )(q3, mk, mv)


def _topk_rows(s, work_ref, rank_ref, sorted_ref):
    nk = s.shape[0]
    iota = lax.broadcasted_iota(jnp.int32, s.shape, 0).astype(F32)
    work_ref[...] = s
    rank_ref[...] = jnp.full(s.shape, float(PEER_TOPK), F32)
    for r in range(PEER_TOPK):
        w = work_ref[...]
        m = jnp.max(w, axis=0, keepdims=True)
        idx = jnp.min(jnp.where(w == m, iota, float(nk)), axis=0, keepdims=True)
        sel = iota == idx
        rank_ref[...] = jnp.where(sel, float(r), rank_ref[...])
        work_ref[...] = jnp.where(sel, NEG_INF, w)
        sorted_ref[r:r + 1, :] = m


def _route_kernel(q_ref, keys_ref, e1_ref, n1_ref, e2_ref, r2_ref,
                  work_ref, rank1_ref, rank2_ref, s1s_ref, s2s_ref):
    q = q_ref[...]
    st = []
    for c in range(2):
        qc = q[:, c * PEER_HALF:(c + 1) * PEER_HALF].astype(BF16)
        st.append(_dot_nt(keys_ref[c, 0].astype(BF16), qc))
    _topk_rows(st[0], work_ref, rank1_ref, s1s_ref)
    _topk_rows(st[1], work_ref, rank2_ref, s2s_ref)
    s1 = s1s_ref[...]
    s2 = s2s_ref[...]

    iota = lax.broadcasted_iota(jnp.int32, s1.shape, 0).astype(F32)
    ptr = jnp.zeros(s1.shape, F32)
    count = jnp.zeros(s1.shape, F32)
    front = s1 + s2[0:1, :]
    zsum = jnp.zeros((1, s1.shape[1]), F32)
    top0 = None
    for kk in range(PEER_TOPK):
        m = jnp.max(front, axis=0, keepdims=True)
        if kk == 0:
            top0 = m
        zsum = zsum + jnp.exp(m - top0)
        istar = jnp.min(jnp.where(front == m, iota, float(PEER_TOPK)), axis=0, keepdims=True)
        oh = iota == istar
        count = count + jnp.where(oh, 1.0, 0.0)
        pnew = jnp.sum(jnp.where(oh, ptr, 0.0), axis=0, keepdims=True) + 1.0
        ptr = jnp.where(oh, pnew, ptr)
        s2n = jnp.max(jnp.where(iota == pnew, s2, NEG_INF), axis=0, keepdims=True)
        s1sel = jnp.max(jnp.where(oh, s1, NEG_INF), axis=0, keepdims=True)
        front = jnp.where(oh, s1sel + s2n, front)

    rank1 = rank1_ref[...]
    rank2 = rank2_ref[...]
    n1 = jnp.zeros(rank1.shape, F32)
    for r in range(PEER_TOPK):
        n1 = jnp.where(rank1 == float(r), count[r:r + 1, :], n1)
    inv_z = 1.0 / zsum
    e1_ref[0] = jnp.where(rank1 < PEER_TOPK, jnp.exp(st[0] - s1[0:1, :]) * inv_z, 0.0)
    e2_ref[0] = jnp.where(rank2 < PEER_TOPK, jnp.exp(st[1] - s2[0:1, :]), 0.0).astype(e2_ref.dtype)
    n1_ref[0] = n1
    r2_ref[0] = rank2.astype(r2_ref.dtype)


def _peer_route(qp, peer_keys, *, tt=256):
    t = qp.shape[0]
    tt = min(tt, t)
    hw = 2 * PEER_HALF
    wide = jax.ShapeDtypeStruct((PEER_HEADS, PEER_NKEYS, t), F32)
    narrow = jax.ShapeDtypeStruct((PEER_HEADS, PEER_NKEYS, t), BF16)
    ospec = pl.BlockSpec((1, PEER_NKEYS, tt), lambda i, h: (h, 0, i))
    return pl.pallas_call(
        _route_kernel,
        grid=(t // tt, PEER_HEADS),
        in_specs=[pl.BlockSpec((tt, hw), lambda i, h: (i, h)),
                  pl.BlockSpec((2, 1, PEER_NKEYS, PEER_HALF), lambda i, h: (0, h, 0, 0))],
        out_specs=[ospec, ospec, ospec, ospec],
        out_shape=[wide, wide, narrow, narrow],
        scratch_shapes=[pltpu.VMEM((PEER_NKEYS, tt), F32), pltpu.VMEM((PEER_NKEYS, tt), F32),
                        pltpu.VMEM((PEER_NKEYS, tt), F32),
                        pltpu.VMEM((PEER_TOPK, tt), F32), pltpu.VMEM((PEER_TOPK, tt), F32)],
        compiler_params=_cparams("parallel", "parallel"),
        name="peer_route",
    )(qp, peer_keys)


def _peer_dense_kernel(h2_ref, gf_ref, gl_ref, u_ref, vt_ref, e1_ref, n1_ref, e2_ref, r2_ref, y_ref,
                       xn_ref, acc_ref, ht_ref, *, et, n_tiles):
    s = pl.program_id(1)
    slabs = et // PEER_NKEYS

    def project():
        ht_ref[lax.rem(s, 2)] = _dot_nt(u_ref[...], xn_ref[...])

    def finish(ht):
        act = jax.nn.gelu(ht).astype(BF16)
        ws = []
        for sl in range(slabs):
            a = (s - 1) * slabs + sl
            w = None
            for h in range(PEER_HEADS):
                e1row = e1_ref[h, pl.ds(a, 1), :].astype(BF16)
                n1row = n1_ref[h, pl.ds(a, 1), :].astype(BF16)
                term = jnp.where(r2_ref[h] < n1row, e2_ref[h], 0.0) * e1row
                w = term if w is None else w + term
            ws.append(w)
        wt = ws[0] if slabs == 1 else jnp.concatenate(ws, axis=0)
        acc_ref[...] += _dot(vt_ref[...], act * wt)

    @pl.when(s == 0)
    def _():
        x = h2_ref[...]
        ms = jnp.mean(x * x, axis=-1, keepdims=True)
        xn_ref[...] = (x * lax.rsqrt(ms + EPS) * gf_ref[...]).astype(BF16)
        acc_ref[...] = jnp.zeros_like(acc_ref)
        project()

    @pl.when(jnp.logical_and(s > 0, s < n_tiles))
    def _():
        ht = ht_ref[lax.rem(s + 1, 2)]
        project()
        finish(ht)

    @pl.when(s == n_tiles)
    def _():
        finish(ht_ref[lax.rem(s + 1, 2)])
        r = h2_ref[...] + acc_ref[...].T
        ms = jnp.mean(r * r, axis=-1, keepdims=True)
        y_ref[...] = r * lax.rsqrt(ms + EPS) * gl_ref[...]


def _peer_dense(h2, g_ffn, g_final, u, vt, route, *, tt=512, et=512):
    t = h2.shape[0]
    tt = min(tt, t)
    n_tiles = u.shape[0] // et
    rspec = pl.BlockSpec((PEER_HEADS, PEER_NKEYS, tt), lambda i, s: (0, 0, i))
    return pl.pallas_call(
        functools.partial(_peer_dense_kernel, et=et, n_tiles=n_tiles),
        grid=(t // tt, n_tiles + 1),
        in_specs=[pl.BlockSpec((tt, D_MODEL), lambda i, s: (i, 0)),
                  pl.BlockSpec((1, D_MODEL), lambda i, s: (0, 0)),
                  pl.BlockSpec((1, D_MODEL), lambda i, s: (0, 0)),
                  pl.BlockSpec((et, D_MODEL), lambda i, s: (jnp.minimum(s, n_tiles - 1), 0)),
                  pl.BlockSpec((D_MODEL, et), lambda i, s: (0, jnp.maximum(s - 1, 0))),
                  rspec, rspec, rspec, rspec],
        out_specs=pl.BlockSpec((tt, D_MODEL), lambda i, s: (i, 0)),
        out_shape=jax.ShapeDtypeStruct((t, D_MODEL), F32),
        scratch_shapes=[pltpu.VMEM((tt, D_MODEL), BF16), pltpu.VMEM((D_MODEL, tt), F32),
                        pltpu.VMEM((2, et, tt), F32)],
        compiler_params=_cparams("parallel", "arbitrary", vmem=PEER_VMEM_LIMIT),
        name="peer_dense",
    )(h2, g_ffn.reshape(1, D_MODEL), g_final.reshape(1, D_MODEL), u, vt, *route)


def _tail(h1, mem_k, mem_v, w, batch):
    t = h1.shape[0]
    qx = _matmul(h1, w["xq"], gain=w["g_cross"], name="xq_proj")
    ox = _cross_attention(qx.reshape(batch, t // batch, X_HEADS * X_DH), mem_k, mem_v)
    h2 = _matmul(ox.reshape(t, X_HEADS * X_DH), w["xo"], residual=h1, name="xo_proj")
    qp = _matmul(h2, w["pq"], gain=w["g_ffn"], name="peer_query")
    route = _peer_route(qp, w["peer_keys"])
    return _peer_dense(h2, w["g_ffn"], w["g_final"], w["peer_u"], w["peer_vt"], route)


def _mixer_out(x2, z2, ret_o, diff_o, w):
    m = _mixer_gate(z2, ret_o, diff_o, w["ret_norm_g"], w["diff_norm_g"], w["branch_a"], w["branch_b"])
    return _matmul(m, w["out"], residual=x2, name="out_proj")


def kernel(x_prompt, x_sample, cache_k, cache_v, state_ret, cache_mem_k, cache_mem_v, page_table, mem_prompt, g_mix, w_in, ret_norm_g, diff_norm_g, lambda_q1, lambda_k1, lambda_q2, lambda_k2, w_branch_a, w_branch_b, w_out, g_cross, w_xq, w_mem_kv, w_xo, g_ffn, w_pq, peer_keys, peer_u, peer_v, g_final):
    bp, sp, d = x_prompt.shape
    bd, td, _ = x_sample.shape
    w = dict(
        ret_norm_g=ret_norm_g, diff_norm_g=diff_norm_g, g_cross=g_cross, g_ffn=g_ffn, g_final=g_final,
        branch_a=w_branch_a.astype(BF16), branch_b=w_branch_b.astype(BF16), out=w_out.astype(BF16),
        xq=w_xq.astype(BF16), xo=w_xo.astype(BF16), pq=w_pq.astype(BF16), peer_keys=peer_keys,
        peer_u=peer_u.astype(BF16), peer_vt=peer_v.astype(BF16).T)
    w_in_b = w_in.astype(BF16)
    lams = [a.reshape(1, DIFF_DH).astype(F32) for a in (lambda_q1, lambda_k1, lambda_q2, lambda_k2)]

    xp2 = x_prompt.reshape(bp * sp, d)
    z2 = _matmul(xp2, w_in_b, gain=g_mix, name="in_proj")
    z3 = z2.reshape(bp, sp, IN_W)
    ret_o, ret_state_prompt = _retention(z3, jnp.arange(sp, dtype=jnp.int32), None, RET_CHUNK, "retention_prompt")
    diff_o = _diff_prompt(z3, lams)
    h1 = _mixer_out(xp2, z2, ret_o.reshape(bp * sp, RET_W), diff_o.reshape(bp * sp, DIFF_W), w)
    mem_kv = _matmul(mem_prompt.reshape(bp * MEM_LEN, d), w_mem_kv.astype(BF16), name="mem_kv_proj")
    mem_kv = mem_kv.reshape(bp, MEM_LEN, 2, X_HEADS, X_DH)
    mem_k_prompt, mem_v_prompt = mem_kv[:, :, 0], mem_kv[:, :, 1]
    y_prompt = _tail(h1, mem_k_prompt, mem_v_prompt, w, bp).reshape(bp, sp, d)
    k_prompt = z3[:, :, OFF_DK:OFF_DK + DIFF_W].reshape(bp, sp, DIFF_HEADS, DIFF_HW)
    v_prompt = z3[:, :, OFF_DV:OFF_DV + DIFF_W].reshape(bp, sp, DIFF_HEADS, DIFF_HW)

    past = page_table.shape[1] * PAGE_SIZE
    xs2 = x_sample.reshape(bd * td, d)
    zs2 = _matmul(xs2, w_in_b, gain=g_mix, name="in_proj")
    zs3 = zs2.reshape(bd, td, IN_W)
    ret_o_s, ret_state_sample = _retention(zs3, past + jnp.arange(td, dtype=jnp.int32), state_ret, td,
                                           "retention_sample")
    diff_o_s = _diff_sample(zs3, cache_k, cache_v, page_table, lams)
    h1s = _mixer_out(xs2, zs2, ret_o_s.reshape(bd * td, RET_W), diff_o_s.reshape(bd * td, DIFF_W), w)
    y_sample = _tail(h1s, cache_mem_k, cache_mem_v, w, bd).reshape(bd, td, d)
    k_sample = zs3[:, :, OFF_DK:OFF_DK + DIFF_W].reshape(bd, td, DIFF_HEADS, DIFF_HW)
    v_sample = zs3[:, :, OFF_DV:OFF_DV + DIFF_W].reshape(bd, td, DIFF_HEADS, DIFF_HW)

    return (y_prompt, y_sample, k_prompt, v_prompt, ret_state_prompt, mem_k_prompt, mem_v_prompt,
            k_sample, v_sample, ret_state_sample)
```

```python
import functools
import math

import numpy as np
import jax
import jax.numpy as jnp
from jax import lax
from jax.experimental import pallas as pl
from jax.experimental.pallas import tpu as pltpu

F32 = jnp.float32
BF16 = jnp.bfloat16

D_MODEL = 2048
RET_HEADS = 8
RET_DK = 128
RET_DV = 128
RET_CHUNK = 128
DIFF_HEADS = 4
DIFF_DH = 128
DIFF_HW = 2 * DIFF_DH
PAGE_SIZE = 128
MEM_LEN = 256
X_HEADS = 4
X_DH = 128
PEER_HEADS = 8
PEER_NKEYS = 128
PEER_HALF = 128
PEER_TOPK = 16
ROPE_BASE = 10000.0
LAMBDA_INIT = 0.8 - 0.6 * math.exp(-0.3 * 0)
EPS = 1e-6

RET_W = RET_HEADS * RET_DK
DIFF_W = DIFF_HEADS * DIFF_HW
OFF_RQ, OFF_RK, OFF_RV, OFF_RG = 0, RET_W, 2 * RET_W, 3 * RET_W
OFF_DQ = 4 * RET_W
OFF_DK = OFF_DQ + DIFF_W
OFF_DV = OFF_DK + DIFF_W
OFF_GA = OFF_DV + DIFF_W
OFF_GB = OFF_GA + D_MODEL
IN_W = OFF_GB + D_MODEL

LANES = 128
VMEM_LIMIT = 48 * 1024 * 1024
PEER_VMEM_LIMIT = 58 * 1024 * 1024
SAMPLE_PAGES_PER_STEP = 8

NEG_INF = float("-inf")


def _cparams(*sem, vmem=VMEM_LIMIT, flags=None):
    return pltpu.CompilerParams(dimension_semantics=sem, vmem_limit_bytes=vmem, flags=flags)


def _dot(a, b):
    return jnp.dot(a, b, preferred_element_type=F32)


def _dot_nt(a, b):
    return lax.dot_general(a, b, (((1,), (1,)), ((), ())), preferred_element_type=F32)


def _dot_tn(a, b):
    return lax.dot_general(a, b, (((0,), (0,)), ((), ())), preferred_element_type=F32)


def _mm_kernel(*refs, norm, residual):
    it = iter(refs)
    x_ref = next(it)
    g_ref = next(it) if norm else None
    w_ref = next(it)
    r_ref = next(it) if residual else None
    o_ref = next(it)
    xs_ref = next(it)

    @pl.when(pl.program_id(1) == 0)
    def _():
        x = x_ref[...].astype(F32)
        if norm:
            ms = jnp.mean(x * x, axis=-1, keepdims=True)
            x = x * lax.rsqrt(ms + EPS) * g_ref[...]
        xs_ref[...] = x.astype(BF16)

    acc = _dot(xs_ref[...], w_ref[...])
    if residual:
        acc = r_ref[...] + acc
    o_ref[...] = acc.astype(o_ref.dtype)


def _matmul(x, w, *, name, gain=None, residual=None, out_dtype=F32, bm=1024, bn=512):
    m, k = x.shape
    n = w.shape[1]
    bm = min(bm, m)
    bn = min(bn, n)
    assert m % bm == 0 and n % bn == 0
    norm = gain is not None
    res = residual is not None
    in_specs = [pl.BlockSpec((bm, k), lambda i, j: (i, 0))]
    args = [x]
    if norm:
        in_specs.append(pl.BlockSpec((1, k), lambda i, j: (0, 0)))
        args.append(gain.reshape(1, k).astype(F32))
    in_specs.append(pl.BlockSpec((k, bn), lambda i, j: (0, j)))
    args.append(w)
    if res:
        in_specs.append(pl.BlockSpec((bm, bn), lambda i, j: (i, j)))
        args.append(residual)
    return pl.pallas_call(
        functools.partial(_mm_kernel, norm=norm, residual=res),
        grid=(m // bm, n // bn),
        in_specs=in_specs,
        out_specs=pl.BlockSpec((bm, bn), lambda i, j: (i, j)),
        out_shape=jax.ShapeDtypeStruct((m, n), out_dtype),
        scratch_shapes=[pltpu.VMEM((bm, k), BF16)],
        compiler_params=_cparams("parallel", "arbitrary"),
        name=name,
    )(*args)


def _ret_tables(chunk_len):
    c = RET_CHUNK
    lg = jnp.log1p(-jnp.exp2(-5.0 - jnp.arange(RET_HEADS, dtype=F32)))
    i = jnp.arange(c, dtype=F32)
    rel = i[:, None] - i[None, :]
    causal = rel >= 0
    dmat = jnp.where(causal[None], jnp.exp(jnp.where(causal, rel, 0.0)[None] * lg[:, None, None]), 0.0)
    dec_in = jnp.exp((i + 1.0)[None, :] * lg[:, None])
    dec_out = jnp.exp((chunk_len - 1.0 - i)[None, :] * lg[:, None])
    dec_chunk = jnp.exp(chunk_len * lg)
    ones = jnp.ones((RET_HEADS, c, LANES), F32)
    return (dmat, dec_in[:, :, None] * ones, dec_out[:, :, None] * ones,
            dec_chunk[:, None, None] * ones)


def _rope_tables(pos):
    half = RET_DK // 2
    freqs = jnp.exp(-math.log(ROPE_BASE) * jnp.arange(half, dtype=F32) / half)
    ang = pos.astype(F32)[:, None] * freqs[None, :]
    cos, sin = jnp.cos(ang), jnp.sin(ang)
    return jnp.concatenate([cos, cos], axis=-1), jnp.concatenate([-sin, sin], axis=-1)


def _ret_kernel(*refs, rows, has_state):
    it = iter(refs)
    q_ref, k_ref, v_ref, cos_ref, sin_ref = (next(it) for _ in range(5))
    dmat_ref, din_ref, dout_ref, dch_ref = (next(it) for _ in range(4))
    s0_ref = next(it) if has_state else None
    o_ref, sfin_ref, s_ref = next(it), next(it), next(it)
    pad_ref = next(it) if rows < RET_CHUNK else None
    n = pl.program_id(1)

    @pl.when(n == 0)
    def _():
        if has_state:
            s_ref[...] = s0_ref[0]
        else:
            s_ref[...] = jnp.zeros_like(s_ref)

    def full(x):
        if rows == RET_CHUNK:
            return x
        width = x.shape[1]
        pad_ref[:, 0:width] = jnp.zeros((RET_CHUNK, width), F32)
        pad_ref[0:rows, 0:width] = x
        return pad_ref[:, 0:width]

    def rot(x, cos, sin):
        return x * cos + pltpu.roll(x, RET_DK // 2, axis=1) * sin

    cos, sin = full(cos_ref[...]), full(sin_ref[...])
    q_all, k_all, v_all = full(q_ref[0]), full(k_ref[0]), full(v_ref[0])
    for h in range(RET_HEADS):
        sl = slice(h * RET_DK, (h + 1) * RET_DK)
        q = rot(q_all[:, sl], cos, sin)
        k = rot(k_all[:, sl], cos, sin) * (RET_DK ** -0.5)
        v = v_all[:, sl].astype(BF16)
        s_prev = s_ref[h]
        qb = q.astype(BF16)
        scores = _dot_nt(qb, k.astype(BF16)) * dmat_ref[h]
        out = _dot(scores.astype(BF16), v) + _dot(qb, s_prev.astype(BF16)) * din_ref[h]
        s_ref[h] = dch_ref[h] * s_prev + _dot_tn((k * dout_ref[h]).astype(BF16), v)
        o_ref[0, :, sl] = out if rows == RET_CHUNK else out[0:rows, :]

    @pl.when(n == pl.num_programs(1) - 1)
    def _():
        sfin_ref[0] = s_ref[...]


def _retention(z3, pos, state0, chunk_len, name):
    b, s, _ = z3.shape
    rows = min(s, RET_CHUNK)
    nchunks = s // rows
    cos, sin = _rope_tables(pos)
    dmat, din, dout, dch = _ret_tables(float(chunk_len))
    has_state = state0 is not None
    qkv_spec = lambda off: pl.BlockSpec((1, rows, RET_W), lambda bi, n: (bi, n, off // RET_W))
    tab_spec = pl.BlockSpec((RET_HEADS, RET_CHUNK, LANES), lambda bi, n: (0, 0, 0))
    state_spec = pl.BlockSpec((1, RET_HEADS, RET_DK, RET_DV), lambda bi, n: (bi, 0, 0, 0))
    in_specs = [qkv_spec(OFF_RQ), qkv_spec(OFF_RK), qkv_spec(OFF_RV),
                pl.BlockSpec((rows, LANES), lambda bi, n: (n, 0)),
                pl.BlockSpec((rows, LANES), lambda bi, n: (n, 0)),
                tab_spec, tab_spec, tab_spec, tab_spec]
    args = [z3, z3, z3, cos, sin, dmat, din, dout, dch]
    if has_state:
        in_specs.append(state_spec)
        args.append(state0)
    scratch = [pltpu.VMEM((RET_HEADS, RET_DK, RET_DV), F32)]
    if rows < RET_CHUNK:
        scratch.append(pltpu.VMEM((RET_CHUNK, RET_W), F32))
    return pl.pallas_call(
        functools.partial(_ret_kernel, rows=rows, has_state=has_state),
        grid=(b, nchunks),
        in_specs=in_specs,
        out_specs=[pl.BlockSpec((1, rows, RET_W), lambda bi, n: (bi, n, 0)), state_spec],
        out_shape=[jax.ShapeDtypeStruct((b, s, RET_W), F32),
                   jax.ShapeDtypeStruct((b, RET_HEADS, RET_DK, RET_DV), F32)],
        scratch_shapes=scratch,
        compiler_params=_cparams("parallel", "arbitrary"),
        name=name,
    )(*args)


def _lambda_value(lq1_ref, lk1_ref, lq2_ref, lk2_ref):
    a = jnp.sum(lq1_ref[...] * lk1_ref[...], axis=-1, keepdims=True)
    b = jnp.sum(lq2_ref[...] * lk2_ref[...], axis=-1, keepdims=True)
    return jnp.exp(a) - jnp.exp(b) + LAMBDA_INIT


def _online_update(s, pv, m_ref, l_ref, acc_ref, idx):
    def lanes(x, width):
        return x if width == LANES else jnp.concatenate([x] * (width // LANES), axis=1)

    m_prev = m_ref[idx]
    m_new = jnp.maximum(m_prev, jnp.max(s, axis=-1, keepdims=True))
    alpha = jnp.exp(m_prev - m_new)
    p = jnp.exp(s - lanes(m_new, s.shape[1]))
    l_ref[idx] = alpha * l_ref[idx] + jnp.sum(p, axis=-1, keepdims=True)
    acc = acc_ref[idx]
    acc_ref[idx] = lanes(alpha, acc.shape[1]) * acc + pv(p)
    m_ref[idx] = m_new


def _diff_prompt_kernel(qi_ref, ki_ref, q_ref, k_ref, v_ref, lq1, lk1, lq2, lk2, o_ref,
                        m_ref, l_ref, acc_ref):
    pair = pl.program_id(2)
    qi, ki = qi_ref[pair], ki_ref[pair]
    scale = DIFF_DH ** -0.5

    @pl.when(ki == 0)
    def _():
        m_ref[...] = jnp.full_like(m_ref, NEG_INF)
        l_ref[...] = jnp.zeros_like(l_ref)
        acc_ref[...] = jnp.zeros_like(acc_ref)

    def step(masked):
        q = q_ref[0]
        k = k_ref[0]
        v = v_ref[0].astype(BF16)
        for c in range(2):
            sl = slice(c * DIFF_DH, (c + 1) * DIFF_DH)
            s = _dot_nt(q[:, sl].astype(BF16), k[:, sl].astype(BF16)) * scale
            if masked:
                row = lax.broadcasted_iota(jnp.int32, s.shape, 0)
                col = lax.broadcasted_iota(jnp.int32, s.shape, 1)
                s = jnp.where(row >= col, s, NEG_INF)
            _online_update(s, lambda p: _dot(p.astype(BF16), v), m_ref, l_ref, acc_ref, c)

    @pl.when(ki < qi)
    def _():
        step(False)

    @pl.when(ki == qi)
    def _():
        step(True)
        lam = _lambda_value(lq1, lk1, lq2, lk2)
        o0 = acc_ref[0] / l_ref[0][:, 0:1]
        o1 = acc_ref[1] / l_ref[1][:, 0:1]
        o_ref[0] = o0 - lam * o1


def _diff_prompt(z3, lams, *, tq=512):
    b, s, _ = z3.shape
    nq = s // tq
    pairs = [(qi, ki) for qi in range(nq) for ki in range(qi + 1)]
    qi_tab = jnp.asarray([p[0] for p in pairs], jnp.int32)
    ki_tab = jnp.asarray([p[1] for p in pairs], jnp.int32)
    col = lambda off: off // DIFF_HW
    lam_spec = pl.BlockSpec((1, DIFF_DH), lambda bi, h, p, qt, kt: (0, 0))
    grid_spec = pltpu.PrefetchScalarGridSpec(
        num_scalar_prefetch=2,
        grid=(b, DIFF_HEADS, len(pairs)),
        in_specs=[
            pl.BlockSpec((1, tq, DIFF_HW), lambda bi, h, p, qt, kt: (bi, qt[p], col(OFF_DQ) + h)),
            pl.BlockSpec((1, tq, DIFF_HW), lambda bi, h, p, qt, kt: (bi, kt[p], col(OFF_DK) + h)),
            pl.BlockSpec((1, tq, DIFF_HW), lambda bi, h, p, qt, kt: (bi, kt[p], col(OFF_DV) + h)),
            lam_spec, lam_spec, lam_spec, lam_spec],
        out_specs=pl.BlockSpec((1, tq, DIFF_HW), lambda bi, h, p, qt, kt: (bi, qt[p], h)),
        scratch_shapes=[pltpu.VMEM((2, tq, LANES), F32), pltpu.VMEM((2, tq, LANES), F32),
                        pltpu.VMEM((2, tq, DIFF_HW), F32)])
    return pl.pallas_call(
        _diff_prompt_kernel,
        grid_spec=grid_spec,
        out_shape=jax.ShapeDtypeStruct((b, s, DIFF_W), F32),
        compiler_params=_cparams("parallel", "parallel", "arbitrary"),
        name="diff_prompt",
    )(qi_tab, ki_tab, z3, z3, z3, *lams)


def _diff_sample_kernel(pt_ref, q_ref, kn_ref, vn_ref, *rest, t, g):
    k_refs, v_refs = rest[:g], rest[g:2 * g]
    lq1, lk1, lq2, lk2, o_ref, qm_ref, m_ref, l_ref, acc_ref, pad_ref = rest[2 * g:]
    p = pl.program_id(1)
    scale = DIFF_DH ** -0.5
    hrows = 2 * t
    nrow = DIFF_HEADS * hrows
    slots = 2 * DIFF_HEADS
    prow = PAGE_SIZE * slots
    slot_bits, hrow_bits = slots.bit_length() - 1, hrows.bit_length() - 1

    @pl.when(p == 0)
    def _():
        m_ref[...] = jnp.full_like(m_ref, NEG_INF)
        l_ref[...] = jnp.zeros_like(l_ref)
        acc_ref[...] = jnp.zeros_like(acc_ref)
        q = q_ref[0]
        for h in range(DIFF_HEADS):
            for c in range(2):
                r0 = h * hrows + c * t
                lo = h * DIFF_HW + c * DIFF_DH
                qm_ref[r0:r0 + t, :] = q[:, lo:lo + DIFF_DH]

    qm = qm_ref[...].astype(BF16)
    row = lax.broadcasted_iota(jnp.int32, (nrow, prow), 0)
    col = lax.broadcasted_iota(jnp.int32, (nrow, prow), 1)
    row_half = jnp.where(jnp.bitwise_and(row, hrows - 1) >= t, DIFF_HEADS, 0)
    row_slot = row_half + lax.shift_right_logical(row, hrow_bits)
    match = jnp.bitwise_and(col, slots - 1) == row_slot

    def page_scores(kpage):
        s = _dot_nt(qm, kpage.astype(BF16)) * scale
        return jnp.where(match, s, NEG_INF)

    def page_values(vpage):
        v3 = vpage.reshape(PAGE_SIZE, slots, DIFF_DH)
        other = pltpu.roll(v3, DIFF_HEADS, 1).reshape(prow, DIFF_DH)
        return jnp.concatenate([vpage.astype(BF16), other.astype(BF16)], axis=1)

    def update(scores, values):
        def pv(pr):
            out = None
            for i, vm in enumerate(values):
                o = _dot(pr[:, i * prow:(i + 1) * prow].astype(BF16), vm)
                out = o if out is None else out + o
            return out
        s = scores[0] if len(scores) == 1 else jnp.concatenate(scores, axis=1)
        _online_update(s, pv, m_ref, l_ref, acc_ref, slice(None))

    update([page_scores(k[0]) for k in k_refs], [page_values(v[0]) for v in v_refs])

    @pl.when(p == pl.num_programs(1) - 1)
    def _():
        pad_ref[...] = jnp.zeros_like(pad_ref)
        kn, vn = kn_ref[0], vn_ref[0]
        for tok in range(t):
            for c in range(2):
                for h in range(DIFF_HEADS):
                    r = tok * slots + c * DIFF_HEADS + h
                    lo = h * DIFF_HW + c * DIFF_DH
                    pad_ref[0, r:r + 1, :] = kn[tok:tok + 1, lo:lo + DIFF_DH]
                    pad_ref[1, r:r + 1, :] = vn[tok:tok + 1, lo:lo + DIFF_DH]
        s = page_scores(pad_ref[0])
        causal = lax.shift_right_logical(col, slot_bits) <= jnp.bitwise_and(row, t - 1)
        update([jnp.where(causal, s, NEG_INF)], [page_values(pad_ref[1])])
        lam = _lambda_value(lq1, lk1, lq2, lk2)
        acc = acc_ref[...]
        swapped = jnp.concatenate([acc[:, DIFF_DH:], acc[:, :DIFF_DH]], axis=1)
        out_row = lax.broadcasted_iota(jnp.int32, acc.shape, 0)
        second_half = jnp.bitwise_and(out_row, hrows - 1) >= t
        o = jnp.where(second_half, swapped, acc) / l_ref[...][:, 0:1]
        for h in range(DIFF_HEADS):
            r0 = h * hrows
            o_ref[0, :, h * DIFF_HW:(h + 1) * DIFF_HW] = o[r0:r0 + t] - lam * o[r0 + t:r0 + hrows]


def _page_rows(cache):
    n_pool = cache.shape[0]
    c5 = cache.reshape(n_pool, PAGE_SIZE, DIFF_HEADS, 2, DIFF_DH)
    return jnp.transpose(c5, (0, 1, 3, 2, 4)).reshape(n_pool, PAGE_SIZE * 2 * DIFF_HEADS, DIFF_DH)


def _diff_sample(zs3, cache_k, cache_v, page_table, lams):
    b, t, _ = zs3.shape
    assert t & (t - 1) == 0
    n_pages = page_table.shape[1]
    g = math.gcd(SAMPLE_PAGES_PER_STEP, n_pages)
    nrow = DIFF_HEADS * 2 * t
    prow = PAGE_SIZE * 2 * DIFF_HEADS
    zcol = lambda off: (lambda bi, p, pt: (bi, 0, off // DIFF_W))
    page = lambda i: (lambda bi, p, pt: (pt[bi, p * g + i], 0, 0))
    page_spec = lambda i: pl.BlockSpec((1, prow, DIFF_DH), page(i))
    lam_spec = pl.BlockSpec((1, DIFF_DH), lambda bi, p, pt: (0, 0))
    grid_spec = pltpu.PrefetchScalarGridSpec(
        num_scalar_prefetch=1,
        grid=(b, n_pages // g),
        in_specs=[
            pl.BlockSpec((1, t, DIFF_W), zcol(OFF_DQ)),
            pl.BlockSpec((1, t, DIFF_W), zcol(OFF_DK)),
            pl.BlockSpec((1, t, DIFF_W), zcol(OFF_DV)),
            *[page_spec(i) for i in range(g)],
            *[page_spec(i) for i in range(g)],
            lam_spec, lam_spec, lam_spec, lam_spec],
        out_specs=pl.BlockSpec((1, t, DIFF_W), lambda bi, p, pt: (bi, 0, 0)),
        scratch_shapes=[pltpu.VMEM((nrow, DIFF_DH), F32),
                        pltpu.VMEM((nrow, LANES), F32),
                        pltpu.VMEM((nrow, LANES), F32),
                        pltpu.VMEM((nrow, DIFF_HW), F32),
                        pltpu.VMEM((2, prow, DIFF_DH), F32)])
    ck, cv = _page_rows(cache_k), _page_rows(cache_v)
    return pl.pallas_call(
        functools.partial(_diff_sample_kernel, t=t, g=g),
        grid_spec=grid_spec,
        out_shape=jax.ShapeDtypeStruct((b, t, DIFF_W), F32),
        compiler_params=_cparams("parallel", "arbitrary"),
        name="diff_sample",
    )(page_table, zs3, zs3, zs3, *([ck] * g), *([cv] * g), *lams)


def _mix_kernel(ro_ref, rg_ref, do_ref, ga_ref, gb_ref, gr_ref, gd_ref, wa_ref, wb_ref, m_ref, ry_ref, dy_ref):
    @pl.when(pl.program_id(1) == 0)
    def _():
        for h in range(RET_HEADS):
            sl = slice(h * RET_DV, (h + 1) * RET_DV)
            o = ro_ref[:, sl]
            y = o * lax.rsqrt(jnp.mean(o * o, axis=-1, keepdims=True) + EPS) * gr_ref[:, sl]
            g = rg_ref[:, sl]
            ry_ref[:, sl] = (y * (g * jax.nn.sigmoid(g))).astype(BF16)
        for h in range(DIFF_HEADS):
            sl = slice(h * DIFF_HW, (h + 1) * DIFF_HW)
            o = do_ref[:, sl]
            y = o * lax.rsqrt(jnp.mean(o * o, axis=-1, keepdims=True) + EPS) * gd_ref[...] * (1.0 - LAMBDA_INIT)
            dy_ref[:, sl] = y.astype(BF16)

    pa = _dot(ry_ref[...], wa_ref[...])
    pb = _dot(dy_ref[...], wb_ref[...])
    m = jax.nn.sigmoid(ga_ref[...]) * pa + jax.nn.sigmoid(gb_ref[...]) * pb
    m_ref[...] = m.astype(m_ref.dtype)


def _mixer_gate(z2, ret_o, diff_o, ret_norm_g, diff_norm_g, wa, wb, *, bm=512, bn=512):
    m = z2.shape[0]
    bm = min(bm, m)
    return pl.pallas_call(
        _mix_kernel,
        grid=(m // bm, D_MODEL // bn),
        in_specs=[
            pl.BlockSpec((bm, RET_W), lambda i, j: (i, 0)),
            pl.BlockSpec((bm, RET_W), lambda i, j: (i, OFF_RG // RET_W)),
            pl.BlockSpec((bm, DIFF_W), lambda i, j: (i, 0)),
            pl.BlockSpec((bm, bn), lambda i, j: (i, OFF_GA // bn + j)),
            pl.BlockSpec((bm, bn), lambda i, j: (i, OFF_GB // bn + j)),
            pl.BlockSpec((1, RET_W), lambda i, j: (0, 0)),
            pl.BlockSpec((1, DIFF_HW), lambda i, j: (0, 0)),
            pl.BlockSpec((RET_W, bn), lambda i, j: (0, j)),
            pl.BlockSpec((DIFF_W, bn), lambda i, j: (0, j))],
        out_specs=pl.BlockSpec((bm, bn), lambda i, j: (i, j)),
        out_shape=jax.ShapeDtypeStruct((m, D_MODEL), BF16),
        scratch_shapes=[pltpu.VMEM((bm, RET_W), BF16), pltpu.VMEM((bm, DIFF_W), BF16)],
        compiler_params=_cparams("parallel", "arbitrary"),
        name="mixer_gate",
    )(ret_o, z2, diff_o, z2, z2, ret_norm_g.reshape(1, RET_W), diff_norm_g.reshape(1, DIFF_HW), wa, wb)


def _xattn_kernel(q_ref, k_ref, v_ref, o_ref):
    scale = X_DH ** -0.5
    q = q_ref[0]
    k = k_ref[0]
    v = v_ref[0]
    for h in range(X_HEADS):
        sl = slice(h * X_DH, (h + 1) * X_DH)
        s = _dot_nt(q[:, sl].astype(BF16), k[:, sl].astype(BF16)) * scale
        s = s - jnp.max(s, axis=-1, keepdims=True)
        p = jnp.exp(s)
        p = p / jnp.sum(p, axis=-1, keepdims=True)
        o_ref[0, :, sl] = _dot(p.astype(BF16), v[:, sl].astype(BF16)).astype(o_ref.dtype)


def _cross_attention(q3, mem_k, mem_v, *, tq=512):
    b, s, w = q3.shape
    tq = min(tq, s)
    mk = mem_k.reshape(b, MEM_LEN, w)
    mv = mem_v.reshape(b, MEM_LEN, w)
    return pl.pallas_call(
        _xattn_kernel,
        grid=(b, s // tq),
        in_specs=[pl.BlockSpec((1, tq, w), lambda bi, i: (bi, i, 0)),
                  pl.BlockSpec((1, MEM_LEN, w), lambda bi, i: (bi, 0, 0)),
                  pl.BlockSpec((1, MEM_LEN, w), lambda bi, i: (bi, 0, 0))],
        out_specs=pl.BlockSpec((1, tq, w), lambda bi, i: (bi, i, 0)),
        out_shape=jax.ShapeDtypeStruct((b, s, w), BF16),
        compiler_params=_cparams("parallel", "parallel"),
        name="cross_attention",
    )(q3, mk, mv)


def _topk_rows(s_ref, work_ref, rank_ref, sorted_ref, *, break_ties):
    shape = s_ref.shape
    nk = shape[0]
    iota = lax.broadcasted_iota(jnp.int32, shape, 0).astype(F32)
    work_ref[...] = s_ref[...]
    rank_ref[...] = jnp.full(shape, float(PEER_TOPK), F32)
    for r in range(PEER_TOPK):
        w = work_ref[...]
        m = jnp.max(w, axis=0, keepdims=True)
        sel = w == m
        if break_ties:
            sel = iota == jnp.min(jnp.where(sel, iota, float(nk)), axis=0, keepdims=True)
        rank_ref[...] = jnp.where(sel, float(r), rank_ref[...])
        work_ref[...] = jnp.where(sel, NEG_INF, w)
        sorted_ref[r:r + 1, :] = m


def _topk_exact(s_ref, work_ref, rank_ref, sorted_ref):
    _topk_rows(s_ref, work_ref, rank_ref, sorted_ref, break_ties=False)
    taken = jnp.sum(jnp.where(rank_ref[...] < PEER_TOPK, 1.0, 0.0), axis=0, keepdims=True)
    tied = jnp.max(jnp.where(taken == float(PEER_TOPK), 0.0, 1.0)) > 0.0

    @pl.when(tied)
    def _():
        _topk_rows(s_ref, work_ref, rank_ref, sorted_ref, break_ties=True)


def _route_kernel(q_ref, keys_ref, e1_ref, n1_ref, e2_ref, r2_ref,
                  st_ref, work_ref, rank1_ref, rank2_ref, s1s_ref, s2s_ref):
    q = q_ref[...]
    for c in range(2):
        qc = q[:, c * PEER_HALF:(c + 1) * PEER_HALF].astype(BF16)
        st_ref[c] = _dot_nt(keys_ref[c, 0].astype(BF16), qc)
    _topk_exact(st_ref.at[0], work_ref, rank1_ref, s1s_ref)
    _topk_exact(st_ref.at[1], work_ref, rank2_ref, s2s_ref)
    st = (st_ref[0], st_ref[1])
    s1 = s1s_ref[...]
    s2 = s2s_ref[...]

    iota = lax.broadcasted_iota(jnp.int32, s1.shape, 0).astype(F32)
    ptr = jnp.zeros(s1.shape, F32)
    count = jnp.zeros(s1.shape, F32)
    front = s1 + s2[0:1, :]
    zsum = jnp.zeros((1, s1.shape[1]), F32)
    top0 = None
    for kk in range(PEER_TOPK):
        m = jnp.max(front, axis=0, keepdims=True)
        if kk == 0:
            top0 = m
        zsum = zsum + jnp.exp(m - top0)
        istar = jnp.min(jnp.where(front == m, iota, float(PEER_TOPK)), axis=0, keepdims=True)
        oh = iota == istar
        count = count + jnp.where(oh, 1.0, 0.0)
        pnew = jnp.sum(jnp.where(oh, ptr, 0.0), axis=0, keepdims=True) + 1.0
        ptr = jnp.where(oh, pnew, ptr)
        s2n = jnp.max(jnp.where(iota == pnew, s2, NEG_INF), axis=0, keepdims=True)
        s1sel = jnp.max(jnp.where(oh, s1, NEG_INF), axis=0, keepdims=True)
        front = jnp.where(oh, s1sel + s2n, front)

    rank1 = rank1_ref[...]
    rank2 = rank2_ref[...]
    n1 = jnp.zeros(rank1.shape, F32)
    for r in range(PEER_TOPK):
        n1 = jnp.where(rank1 == float(r), count[r:r + 1, :], n1)
    inv_z = 1.0 / zsum
    e1_ref[0] = jnp.where(rank1 < PEER_TOPK, jnp.exp(st[0] - s1[0:1, :]) * inv_z, 0.0)
    e2_ref[0] = jnp.where(rank2 < PEER_TOPK, jnp.exp(st[1] - s2[0:1, :]), 0.0).astype(e2_ref.dtype)
    n1_ref[0] = n1
    r2_ref[0] = rank2.astype(r2_ref.dtype)


def _peer_route(qp, peer_keys, *, tt=512):
    t = qp.shape[0]
    tt = min(tt, t)
    hw = 2 * PEER_HALF
    wide = jax.ShapeDtypeStruct((PEER_HEADS, PEER_NKEYS, t), F32)
    narrow = jax.ShapeDtypeStruct((PEER_HEADS, PEER_NKEYS, t), BF16)
    ospec = pl.BlockSpec((1, PEER_NKEYS, tt), lambda i, h: (h, 0, i))
    return pl.pallas_call(
        _route_kernel,
        grid=(t // tt, PEER_HEADS),
        in_specs=[pl.BlockSpec((tt, hw), lambda i, h: (i, h)),
                  pl.BlockSpec((2, 1, PEER_NKEYS, PEER_HALF), lambda i, h: (0, h, 0, 0))],
        out_specs=[ospec, ospec, ospec, ospec],
        out_shape=[wide, wide, narrow, narrow],
        scratch_shapes=[pltpu.VMEM((2, PEER_NKEYS, tt), F32),
                        pltpu.VMEM((PEER_NKEYS, tt), F32), pltpu.VMEM((PEER_NKEYS, tt), F32),
                        pltpu.VMEM((PEER_NKEYS, tt), F32),
                        pltpu.VMEM((PEER_TOPK, tt), F32), pltpu.VMEM((PEER_TOPK, tt), F32)],
        compiler_params=_cparams("parallel", "parallel"),
        name="peer_route",
    )(qp, peer_keys)


def _peer_dense_kernel(h2_ref, gf_ref, gl_ref, u_ref, vt_ref, e1_ref, n1_ref, e2_ref, r2_ref, y_ref,
                       xn_ref, acc_ref, ht0_ref, ht1_ref, *, et, n_tiles, th):
    s = pl.program_id(1)
    slabs = et // PEER_NKEYS
    tt = xn_ref.shape[0]
    ht_refs = (ht0_ref, ht1_ref)

    def project(par):
        ht_refs[par][...] = _dot_nt(u_ref[...], xn_ref[...])

    def finish(par):
        for c in range(tt // th):
            cols = slice(c * th, (c + 1) * th)
            act = jax.nn.gelu(ht_refs[par][:, cols]).astype(BF16)
            ws = []
            for sl in range(slabs):
                a = (s - 1) * slabs + sl
                w = None
                for h in range(PEER_HEADS):
                    e1row = e1_ref[h, pl.ds(a, 1), cols].astype(BF16)
                    n1row = n1_ref[h, pl.ds(a, 1), cols].astype(BF16)
                    term = jnp.where(r2_ref[h, :, cols] < n1row, e2_ref[h, :, cols], 0.0) * e1row
                    w = term if w is None else w + term
                ws.append(w)
            wt = ws[0] if slabs == 1 else jnp.concatenate(ws, axis=0)
            acc_ref[:, cols] += _dot(vt_ref[...], act * wt)

    @pl.when(s == 0)
    def _():
        x = h2_ref[...]
        ms = jnp.mean(x * x, axis=-1, keepdims=True)
        xn_ref[...] = (x * lax.rsqrt(ms + EPS) * gf_ref[...]).astype(BF16)
        acc_ref[...] = jnp.zeros_like(acc_ref)
        project(0)

    for par in (0, 1):
        @pl.when(jnp.logical_and(jnp.logical_and(s > 0, s < n_tiles), lax.rem(s, 2) == par))
        def _():
            project(par)
            finish(1 - par)

    @pl.when(s == n_tiles)
    def _():
        finish((n_tiles - 1) % 2)
        r = h2_ref[...] + acc_ref[...].T
        ms = jnp.mean(r * r, axis=-1, keepdims=True)
        y_ref[...] = r * lax.rsqrt(ms + EPS) * gl_ref[...]


def _peer_dense(h2, g_ffn, g_final, u, vt, route, *, tt=512, et=512, th=256):
    t = h2.shape[0]
    tt = min(tt, t)
    n_tiles = u.shape[0] // et
    rspec = pl.BlockSpec((PEER_HEADS, PEER_NKEYS, tt), lambda i, s: (0, 0, i))
    return pl.pallas_call(
        functools.partial(_peer_dense_kernel, et=et, n_tiles=n_tiles, th=min(th, tt)),
        grid=(t // tt, n_tiles + 1),
        in_specs=[pl.BlockSpec((tt, D_MODEL), lambda i, s: (i, 0)),
                  pl.BlockSpec((1, D_MODEL), lambda i, s: (0, 0)),
                  pl.BlockSpec((1, D_MODEL), lambda i, s: (0, 0)),
                  pl.BlockSpec((et, D_MODEL), lambda i, s: (jnp.minimum(s, n_tiles - 1), 0)),
                  pl.BlockSpec((D_MODEL, et), lambda i, s: (0, jnp.maximum(s - 1, 0))),
                  rspec, rspec, rspec, rspec],
        out_specs=pl.BlockSpec((tt, D_MODEL), lambda i, s: (i, 0)),
        out_shape=jax.ShapeDtypeStruct((t, D_MODEL), F32),
        scratch_shapes=[pltpu.VMEM((tt, D_MODEL), BF16), pltpu.VMEM((D_MODEL, tt), F32),
                        pltpu.VMEM((et, tt), F32), pltpu.VMEM((et, tt), F32)],
        compiler_params=_cparams("parallel", "arbitrary", vmem=PEER_VMEM_LIMIT),
        name="peer_dense",
    )(h2, g_ffn.reshape(1, D_MODEL), g_final.reshape(1, D_MODEL), u, vt, *route)


def _tail(h1, mem_k, mem_v, w, batch):
    t = h1.shape[0]
    qx = _matmul(h1, w["xq"], gain=w["g_cross"], name="xq_proj")
    ox = _cross_attention(qx.reshape(batch, t // batch, X_HEADS * X_DH), mem_k, mem_v)
    h2 = _matmul(ox.reshape(t, X_HEADS * X_DH), w["xo"], residual=h1, name="xo_proj")
    qp = _matmul(h2, w["pq"], gain=w["g_ffn"], name="peer_query")
    route = _peer_route(qp, w["peer_keys"])
    return _peer_dense(h2, w["g_ffn"], w["g_final"], w["peer_u"], w["peer_vt"], route)


def _mixer_out(x2, z2, ret_o, diff_o, w):
    m = _mixer_gate(z2, ret_o, diff_o, w["ret_norm_g"], w["diff_norm_g"], w["branch_a"], w["branch_b"])
    return _matmul(m, w["out"], residual=x2, name="out_proj")


def kernel(x_prompt, x_sample, cache_k, cache_v, state_ret, cache_mem_k, cache_mem_v, page_table, mem_prompt, g_mix, w_in, ret_norm_g, diff_norm_g, lambda_q1, lambda_k1, lambda_q2, lambda_k2, w_branch_a, w_branch_b, w_out, g_cross, w_xq, w_mem_kv, w_xo, g_ffn, w_pq, peer_keys, peer_u, peer_v, g_final):
    bp, sp, d = x_prompt.shape
    bd, td, _ = x_sample.shape
    w = dict(
        ret_norm_g=ret_norm_g, diff_norm_g=diff_norm_g, g_cross=g_cross, g_ffn=g_ffn, g_final=g_final,
        branch_a=w_branch_a.astype(BF16), branch_b=w_branch_b.astype(BF16), out=w_out.astype(BF16),
        xq=w_xq.astype(BF16), xo=w_xo.astype(BF16), pq=w_pq.astype(BF16), peer_keys=peer_keys,
        peer_u=peer_u.astype(BF16), peer_vt=peer_v.astype(BF16).T)
    w_in_b = w_in.astype(BF16)
    lams = [a.reshape(1, DIFF_DH).astype(F32) for a in (lambda_q1, lambda_k1, lambda_q2, lambda_k2)]

    xp2 = x_prompt.reshape(bp * sp, d)
    z2 = _matmul(xp2, w_in_b, gain=g_mix, name="in_proj")
    z3 = z2.reshape(bp, sp, IN_W)
    ret_o, ret_state_prompt = _retention(z3, jnp.arange(sp, dtype=jnp.int32), None, RET_CHUNK, "retention_prompt")
    diff_o = _diff_prompt(z3, lams)
    h1 = _mixer_out(xp2, z2, ret_o.reshape(bp * sp, RET_W), diff_o.reshape(bp * sp, DIFF_W), w)
    mem_kv = _matmul(mem_prompt.reshape(bp * MEM_LEN, d), w_mem_kv.astype(BF16), name="mem_kv_proj")
    mem_kv = mem_kv.reshape(bp, MEM_LEN, 2, X_HEADS, X_DH)
    mem_k_prompt, mem_v_prompt = mem_kv[:, :, 0], mem_kv[:, :, 1]
    y_prompt = _tail(h1, mem_k_prompt, mem_v_prompt, w, bp).reshape(bp, sp, d)
    k_prompt = z3[:, :, OFF_DK:OFF_DK + DIFF_W].reshape(bp, sp, DIFF_HEADS, DIFF_HW)
    v_prompt = z3[:, :, OFF_DV:OFF_DV + DIFF_W].reshape(bp, sp, DIFF_HEADS, DIFF_HW)

    past = page_table.shape[1] * PAGE_SIZE
    xs2 = x_sample.reshape(bd * td, d)
    zs2 = _matmul(xs2, w_in_b, gain=g_mix, name="in_proj")
    zs3 = zs2.reshape(bd, td, IN_W)
    ret_o_s, ret_state_sample = _retention(zs3, past + jnp.arange(td, dtype=jnp.int32), state_ret, td,
                                           "retention_sample")
    diff_o_s = _diff_sample(zs3, cache_k, cache_v, page_table, lams)
    h1s = _mixer_out(xs2, zs2, ret_o_s.reshape(bd * td, RET_W), diff_o_s.reshape(bd * td, DIFF_W), w)
    y_sample = _tail(h1s, cache_mem_k, cache_mem_v, w, bd).reshape(bd, td, d)
    k_sample = zs3[:, :, OFF_DK:OFF_DK + DIFF_W].reshape(bd, td, DIFF_HEADS, DIFF_HW)
    v_sample = zs3[:, :, OFF_DV:OFF_DV + DIFF_W].reshape(bd, td, DIFF_HEADS, DIFF_HW)

    return (y_prompt, y_sample, k_prompt, v_prompt, ret_state_prompt, mem_k_prompt, mem_v_prompt,
            k_sample, v_sample, ret_state_sample)
```

```python
import functools
import math

import numpy as np
import jax
import jax.numpy as jnp
from jax import lax
from jax.experimental import pallas as pl
from jax.experimental.pallas import tpu as pltpu

F32 = jnp.float32
BF16 = jnp.bfloat16

D_MODEL = 2048
RET_HEADS = 8
RET_DK = 128
RET_DV = 128
RET_CHUNK = 128
DIFF_HEADS = 4
DIFF_DH = 128
DIFF_HW = 2 * DIFF_DH
PAGE_SIZE = 128
MEM_LEN = 256
X_HEADS = 4
X_DH = 128
PEER_HEADS = 8
PEER_NKEYS = 128
PEER_HALF = 128
PEER_TOPK = 16
ROPE_BASE = 10000.0
LAMBDA_INIT = 0.8 - 0.6 * math.exp(-0.3 * 0)
EPS = 1e-6

RET_W = RET_HEADS * RET_DK
DIFF_W = DIFF_HEADS * DIFF_HW
OFF_RQ, OFF_RK, OFF_RV, OFF_RG = 0, RET_W, 2 * RET_W, 3 * RET_W
OFF_DQ = 4 * RET_W
OFF_DK = OFF_DQ + DIFF_W
OFF_DV = OFF_DK + DIFF_W
OFF_GA = OFF_DV + DIFF_W
OFF_GB = OFF_GA + D_MODEL
IN_W = OFF_GB + D_MODEL

LANES = 128
VMEM_LIMIT = 48 * 1024 * 1024
PEER_VMEM_LIMIT = 58 * 1024 * 1024
PEER_EXPERT_TILE = 512
SAMPLE_PAGES_PER_STEP = 16

NEG_INF = float("-inf")


def _cparams(*sem, vmem=VMEM_LIMIT, flags=None):
    return pltpu.CompilerParams(dimension_semantics=sem, vmem_limit_bytes=vmem, flags=flags)


def _dot(a, b):
    return jnp.dot(a, b, preferred_element_type=F32)


def _dot_nt(a, b):
    return lax.dot_general(a, b, (((1,), (1,)), ((), ())), preferred_element_type=F32)


def _dot_tn(a, b):
    return lax.dot_general(a, b, (((0,), (0,)), ((), ())), preferred_element_type=F32)


def _mm_kernel(*refs, norm, residual):
    it = iter(refs)
    x_ref = next(it)
    g_ref = next(it) if norm else None
    w_ref = next(it)
    r_ref = next(it) if residual else None
    o_ref = next(it)
    xs_ref = next(it)

    @pl.when(pl.program_id(1) == 0)
    def _():
        x = x_ref[...].astype(F32)
        if norm:
            ms = jnp.mean(x * x, axis=-1, keepdims=True)
            x = x * lax.rsqrt(ms + EPS) * g_ref[...]
        xs_ref[...] = x.astype(BF16)

    acc = _dot(xs_ref[...], w_ref[...].astype(BF16))
    if residual:
        acc = r_ref[...] + acc
    o_ref[...] = acc.astype(o_ref.dtype)


def _matmul(x, w, *, name, gain=None, residual=None, out_dtype=F32, bm=1024, bn=512):
    m, k = x.shape
    n = w.shape[1]
    bm = min(bm, m)
    bn = min(bn, n)
    assert m % bm == 0 and n % bn == 0
    norm = gain is not None
    res = residual is not None
    in_specs = [pl.BlockSpec((bm, k), lambda i, j: (i, 0))]
    args = [x]
    if norm:
        in_specs.append(pl.BlockSpec((1, k), lambda i, j: (0, 0)))
        args.append(gain.reshape(1, k).astype(F32))
    in_specs.append(pl.BlockSpec((k, bn), lambda i, j: (0, j)))
    args.append(w)
    if res:
        in_specs.append(pl.BlockSpec((bm, bn), lambda i, j: (i, j)))
        args.append(residual)
    return pl.pallas_call(
        functools.partial(_mm_kernel, norm=norm, residual=res),
        grid=(m // bm, n // bn),
        in_specs=in_specs,
        out_specs=pl.BlockSpec((bm, bn), lambda i, j: (i, j)),
        out_shape=jax.ShapeDtypeStruct((m, n), out_dtype),
        scratch_shapes=[pltpu.VMEM((bm, k), BF16)],
        compiler_params=_cparams("parallel", "arbitrary"),
        name=name,
    )(*args)


def _ret_tables(chunk_len):
    c = RET_CHUNK
    lg = jnp.log1p(-jnp.exp2(-5.0 - jnp.arange(RET_HEADS, dtype=F32)))
    i = jnp.arange(c, dtype=F32)
    rel = i[:, None] - i[None, :]
    causal = rel >= 0
    dmat = jnp.where(causal[None], jnp.exp(jnp.where(causal, rel, 0.0)[None] * lg[:, None, None]), 0.0)
    dec_in = jnp.exp((i + 1.0)[None, :] * lg[:, None])
    dec_out = jnp.exp((chunk_len - 1.0 - i)[None, :] * lg[:, None])
    dec_chunk = jnp.exp(chunk_len * lg)
    ones = jnp.ones((RET_HEADS, c, LANES), F32)
    return (dmat, dec_in[:, :, None] * ones, dec_out[:, :, None] * ones,
            dec_chunk[:, None, None] * ones)


def _rope_tables(pos):
    half = RET_DK // 2
    freqs = jnp.exp(-math.log(ROPE_BASE) * jnp.arange(half, dtype=F32) / half)
    ang = pos.astype(F32)[:, None] * freqs[None, :]
    cos, sin = jnp.cos(ang), jnp.sin(ang)
    return jnp.concatenate([cos, cos], axis=-1), jnp.concatenate([-sin, sin], axis=-1)


def _ret_kernel(*refs, rows, has_state):
    it = iter(refs)
    q_ref, k_ref, v_ref, cos_ref, sin_ref = (next(it) for _ in range(5))
    dmat_ref, din_ref, dout_ref, dch_ref = (next(it) for _ in range(4))
    s0_ref = next(it) if has_state else None
    o_ref, sfin_ref, s_ref = next(it), next(it), next(it)
    pad_ref = next(it) if rows < RET_CHUNK else None
    n = pl.program_id(1)

    @pl.when(n == 0)
    def _():
        if has_state:
            s_ref[...] = s0_ref[0]
        else:
            s_ref[...] = jnp.zeros_like(s_ref)

    def full(x):
        if rows == RET_CHUNK:
            return x
        width = x.shape[1]
        pad_ref[:, 0:width] = jnp.zeros((RET_CHUNK, width), F32)
        pad_ref[0:rows, 0:width] = x
        return pad_ref[:, 0:width]

    def rot(x, cos, sin):
        return x * cos + pltpu.roll(x, RET_DK // 2, axis=1) * sin

    cos, sin = full(cos_ref[...]), full(sin_ref[...])
    q_all, k_all, v_all = full(q_ref[0]), full(k_ref[0]), full(v_ref[0])
    for h in range(RET_HEADS):
        sl = slice(h * RET_DK, (h + 1) * RET_DK)
        q = rot(q_all[:, sl], cos, sin)
        k = rot(k_all[:, sl], cos, sin) * (RET_DK ** -0.5)
        v = v_all[:, sl].astype(BF16)
        s_prev = s_ref[h]
        qb = q.astype(BF16)
        scores = _dot_nt(qb, k.astype(BF16)) * dmat_ref[h]
        out = _dot(scores.astype(BF16), v) + _dot(qb, s_prev.astype(BF16)) * din_ref[h]
        s_ref[h] = dch_ref[h] * s_prev + _dot_tn((k * dout_ref[h]).astype(BF16), v)
        o_ref[0, :, sl] = out if rows == RET_CHUNK else out[0:rows, :]

    @pl.when(n == pl.num_programs(1) - 1)
    def _():
        sfin_ref[0] = s_ref[...]


def _retention(z3, pos, state0, chunk_len, name):
    b, s, _ = z3.shape
    rows = min(s, RET_CHUNK)
    nchunks = s // rows
    cos, sin = _rope_tables(pos)
    dmat, din, dout, dch = _ret_tables(float(chunk_len))
    has_state = state0 is not None
    qkv_spec = lambda off: pl.BlockSpec((1, rows, RET_W), lambda bi, n: (bi, n, off // RET_W))
    tab_spec = pl.BlockSpec((RET_HEADS, RET_CHUNK, LANES), lambda bi, n: (0, 0, 0))
    state_spec = pl.BlockSpec((1, RET_HEADS, RET_DK, RET_DV), lambda bi, n: (bi, 0, 0, 0))
    in_specs = [qkv_spec(OFF_RQ), qkv_spec(OFF_RK), qkv_spec(OFF_RV),
                pl.BlockSpec((rows, LANES), lambda bi, n: (n, 0)),
                pl.BlockSpec((rows, LANES), lambda bi, n: (n, 0)),
                tab_spec, tab_spec, tab_spec, tab_spec]
    args = [z3, z3, z3, cos, sin, dmat, din, dout, dch]
    if has_state:
        in_specs.append(state_spec)
        args.append(state0)
    scratch = [pltpu.VMEM((RET_HEADS, RET_DK, RET_DV), F32)]
    if rows < RET_CHUNK:
        scratch.append(pltpu.VMEM((RET_CHUNK, RET_W), F32))
    return pl.pallas_call(
        functools.partial(_ret_kernel, rows=rows, has_state=has_state),
        grid=(b, nchunks),
        in_specs=in_specs,
        out_specs=[pl.BlockSpec((1, rows, RET_W), lambda bi, n: (bi, n, 0)), state_spec],
        out_shape=[jax.ShapeDtypeStruct((b, s, RET_W), F32),
                   jax.ShapeDtypeStruct((b, RET_HEADS, RET_DK, RET_DV), F32)],
        scratch_shapes=scratch,
        compiler_params=_cparams("parallel", "arbitrary"),
        name=name,
    )(*args)


def _lambda_value(lq1_ref, lk1_ref, lq2_ref, lk2_ref):
    a = jnp.sum(lq1_ref[...] * lk1_ref[...], axis=-1, keepdims=True)
    b = jnp.sum(lq2_ref[...] * lk2_ref[...], axis=-1, keepdims=True)
    return jnp.exp(a) - jnp.exp(b) + LAMBDA_INIT


def _online_update(s, pv, m_ref, l_ref, acc_ref, idx):
    def lanes(x, width):
        return x if width == LANES else jnp.concatenate([x] * (width // LANES), axis=1)

    m_prev = m_ref[idx]
    m_new = jnp.maximum(m_prev, jnp.max(s, axis=-1, keepdims=True))
    alpha = jnp.exp(m_prev - m_new)
    p = jnp.exp(s - lanes(m_new, s.shape[1]))
    l_ref[idx] = alpha * l_ref[idx] + jnp.sum(p, axis=-1, keepdims=True)
    acc = acc_ref[idx]
    acc_ref[idx] = lanes(alpha, acc.shape[1]) * acc + pv(p)
    m_ref[idx] = m_new


def _diff_prompt_kernel(qi_ref, ki_ref, q_ref, k_ref, v_ref, lq1, lk1, lq2, lk2, o_ref,
                        m_ref, l_ref, acc_ref):
    pair = pl.program_id(2)
    qi, ki = qi_ref[pair], ki_ref[pair]
    scale = DIFF_DH ** -0.5

    @pl.when(ki == 0)
    def _():
        m_ref[...] = jnp.full_like(m_ref, NEG_INF)
        l_ref[...] = jnp.zeros_like(l_ref)
        acc_ref[...] = jnp.zeros_like(acc_ref)

    def step(masked):
        q = q_ref[0]
        k = k_ref[0]
        v = v_ref[0].astype(BF16)
        for c in range(2):
            sl = slice(c * DIFF_DH, (c + 1) * DIFF_DH)
            s = _dot_nt(q[:, sl].astype(BF16), k[:, sl].astype(BF16)) * scale
            if masked:
                row = lax.broadcasted_iota(jnp.int32, s.shape, 0)
                col = lax.broadcasted_iota(jnp.int32, s.shape, 1)
                s = jnp.where(row >= col, s, NEG_INF)
            _online_update(s, lambda p: _dot(p.astype(BF16), v), m_ref, l_ref, acc_ref, c)

    @pl.when(ki < qi)
    def _():
        step(False)

    @pl.when(ki == qi)
    def _():
        step(True)
        lam = _lambda_value(lq1, lk1, lq2, lk2)
        o0 = acc_ref[0] / l_ref[0][:, 0:1]
        o1 = acc_ref[1] / l_ref[1][:, 0:1]
        o_ref[0] = o0 - lam * o1


def _diff_prompt(z3, lams, *, tq=512):
    b, s, _ = z3.shape
    nq = s // tq
    pairs = [(qi, ki) for qi in range(nq) for ki in range(qi + 1)]
    qi_tab = jnp.asarray([p[0] for p in pairs], jnp.int32)
    ki_tab = jnp.asarray([p[1] for p in pairs], jnp.int32)
    col = lambda off: off // DIFF_HW
    lam_spec = pl.BlockSpec((1, DIFF_DH), lambda bi, h, p, qt, kt: (0, 0))
    grid_spec = pltpu.PrefetchScalarGridSpec(
        num_scalar_prefetch=2,
        grid=(b, DIFF_HEADS, len(pairs)),
        in_specs=[
            pl.BlockSpec((1, tq, DIFF_HW), lambda bi, h, p, qt, kt: (bi, qt[p], col(OFF_DQ) + h)),
            pl.BlockSpec((1, tq, DIFF_HW), lambda bi, h, p, qt, kt: (bi, kt[p], col(OFF_DK) + h)),
            pl.BlockSpec((1, tq, DIFF_HW), lambda bi, h, p, qt, kt: (bi, kt[p], col(OFF_DV) + h)),
            lam_spec, lam_spec, lam_spec, lam_spec],
        out_specs=pl.BlockSpec((1, tq, DIFF_HW), lambda bi, h, p, qt, kt: (bi, qt[p], h)),
        scratch_shapes=[pltpu.VMEM((2, tq, LANES), F32), pltpu.VMEM((2, tq, LANES), F32),
                        pltpu.VMEM((2, tq, DIFF_HW), F32)])
    return pl.pallas_call(
        _diff_prompt_kernel,
        grid_spec=grid_spec,
        out_shape=jax.ShapeDtypeStruct((b, s, DIFF_W), F32),
        compiler_params=_cparams("parallel", "parallel", "arbitrary"),
        name="diff_prompt",
    )(qi_tab, ki_tab, z3, z3, z3, *lams)


def _diff_sample_kernel(pt_ref, q_ref, kn_ref, vn_ref, *rest, t, g):
    k_refs, v_refs = rest[:g], rest[g:2 * g]
    lq1, lk1, lq2, lk2, o_ref, qm_ref, m_ref, l_ref, acc_ref, pad_ref = rest[2 * g:]
    p = pl.program_id(1)
    scale = DIFF_DH ** -0.5
    hrows = 2 * t
    nrow = DIFF_HEADS * hrows
    slots = 2 * DIFF_HEADS
    prow = PAGE_SIZE * slots
    slot_bits, hrow_bits = slots.bit_length() - 1, hrows.bit_length() - 1

    @pl.when(p == 0)
    def _():
        m_ref[...] = jnp.full_like(m_ref, NEG_INF)
        l_ref[...] = jnp.zeros_like(l_ref)
        acc_ref[...] = jnp.zeros_like(acc_ref)
        q = q_ref[0]
        for h in range(DIFF_HEADS):
            for c in range(2):
                r0 = h * hrows + c * t
                lo = h * DIFF_HW + c * DIFF_DH
                qm_ref[r0:r0 + t, :] = q[:, lo:lo + DIFF_DH]

    qm = qm_ref[...].astype(BF16)
    row = lax.broadcasted_iota(jnp.int32, (nrow, prow), 0)
    col = lax.broadcasted_iota(jnp.int32, (nrow, prow), 1)
    row_half = jnp.where(jnp.bitwise_and(row, hrows - 1) >= t, DIFF_HEADS, 0)
    row_slot = row_half + lax.shift_right_logical(row, hrow_bits)
    match = jnp.bitwise_and(col, slots - 1) == row_slot

    def page_scores(kpage):
        s = _dot_nt(qm, kpage.astype(BF16)) * scale
        return jnp.where(match, s, NEG_INF)

    def page_values(vpage):
        v3 = vpage.reshape(PAGE_SIZE, slots, DIFF_DH)
        other = pltpu.roll(v3, DIFF_HEADS, 1).reshape(prow, DIFF_DH)
        return jnp.concatenate([vpage.astype(BF16), other.astype(BF16)], axis=1)

    def update(scores, values):
        def pv(pr):
            out = None
            for i, vm in enumerate(values):
                o = _dot(pr[:, i * prow:(i + 1) * prow].astype(BF16), vm)
                out = o if out is None else out + o
            return out
        s = scores[0] if len(scores) == 1 else jnp.concatenate(scores, axis=1)
        _online_update(s, pv, m_ref, l_ref, acc_ref, slice(None))

    update([page_scores(k[0]) for k in k_refs], [page_values(v[0]) for v in v_refs])

    @pl.when(p == pl.num_programs(1) - 1)
    def _():
        pad_ref[...] = jnp.zeros_like(pad_ref)
        kn, vn = kn_ref[0], vn_ref[0]
        for tok in range(t):
            for c in range(2):
                for h in range(DIFF_HEADS):
                    r = tok * slots + c * DIFF_HEADS + h
                    lo = h * DIFF_HW + c * DIFF_DH
                    pad_ref[0, r:r + 1, :] = kn[tok:tok + 1, lo:lo + DIFF_DH]
                    pad_ref[1, r:r + 1, :] = vn[tok:tok + 1, lo:lo + DIFF_DH]
        s = page_scores(pad_ref[0])
        causal = lax.shift_right_logical(col, slot_bits) <= jnp.bitwise_and(row, t - 1)
        update([jnp.where(causal, s, NEG_INF)], [page_values(pad_ref[1])])
        lam = _lambda_value(lq1, lk1, lq2, lk2)
        acc = acc_ref[...]
        swapped = jnp.concatenate([acc[:, DIFF_DH:], acc[:, :DIFF_DH]], axis=1)
        out_row = lax.broadcasted_iota(jnp.int32, acc.shape, 0)
        second_half = jnp.bitwise_and(out_row, hrows - 1) >= t
        o = jnp.where(second_half, swapped, acc) / l_ref[...][:, 0:1]
        for h in range(DIFF_HEADS):
            r0 = h * hrows
            o_ref[0, :, h * DIFF_HW:(h + 1) * DIFF_HW] = o[r0:r0 + t] - lam * o[r0 + t:r0 + hrows]


def _page_rows(cache):
    n_pool = cache.shape[0]
    c5 = cache.reshape(n_pool, PAGE_SIZE, DIFF_HEADS, 2, DIFF_DH)
    return jnp.transpose(c5, (0, 1, 3, 2, 4)).reshape(n_pool, PAGE_SIZE * 2 * DIFF_HEADS, DIFF_DH)


def _diff_sample(zs3, cache_k, cache_v, page_table, lams):
    b, t, _ = zs3.shape
    assert t & (t - 1) == 0
    n_pages = page_table.shape[1]
    g = math.gcd(SAMPLE_PAGES_PER_STEP, n_pages)
    nrow = DIFF_HEADS * 2 * t
    prow = PAGE_SIZE * 2 * DIFF_HEADS
    zcol = lambda off: (lambda bi, p, pt: (bi, 0, off // DIFF_W))
    page = lambda i: (lambda bi, p, pt: (pt[bi, p * g + i], 0, 0))
    page_spec = lambda i: pl.BlockSpec((1, prow, DIFF_DH), page(i))
    lam_spec = pl.BlockSpec((1, DIFF_DH), lambda bi, p, pt: (0, 0))
    grid_spec = pltpu.PrefetchScalarGridSpec(
        num_scalar_prefetch=1,
        grid=(b, n_pages // g),
        in_specs=[
            pl.BlockSpec((1, t, DIFF_W), zcol(OFF_DQ)),
            pl.BlockSpec((1, t, DIFF_W), zcol(OFF_DK)),
            pl.BlockSpec((1, t, DIFF_W), zcol(OFF_DV)),
            *[page_spec(i) for i in range(g)],
            *[page_spec(i) for i in range(g)],
            lam_spec, lam_spec, lam_spec, lam_spec],
        out_specs=pl.BlockSpec((1, t, DIFF_W), lambda bi, p, pt: (bi, 0, 0)),
        scratch_shapes=[pltpu.VMEM((nrow, DIFF_DH), F32),
                        pltpu.VMEM((nrow, LANES), F32),
                        pltpu.VMEM((nrow, LANES), F32),
                        pltpu.VMEM((nrow, DIFF_HW), F32),
                        pltpu.VMEM((2, prow, DIFF_DH), F32)])
    ck, cv = _page_rows(cache_k), _page_rows(cache_v)
    return pl.pallas_call(
        functools.partial(_diff_sample_kernel, t=t, g=g),
        grid_spec=grid_spec,
        out_shape=jax.ShapeDtypeStruct((b, t, DIFF_W), F32),
        compiler_params=_cparams("parallel", "arbitrary"),
        name="diff_sample",
    )(page_table, zs3, zs3, zs3, *([ck] * g), *([cv] * g), *lams)


def _mix_kernel(ro_ref, rg_ref, do_ref, ga_ref, gb_ref, gr_ref, gd_ref, wa_ref, wb_ref, m_ref, ry_ref, dy_ref):
    @pl.when(pl.program_id(1) == 0)
    def _():
        for h in range(RET_HEADS):
            sl = slice(h * RET_DV, (h + 1) * RET_DV)
            o = ro_ref[:, sl]
            y = o * lax.rsqrt(jnp.mean(o * o, axis=-1, keepdims=True) + EPS) * gr_ref[:, sl]
            g = rg_ref[:, sl]
            ry_ref[:, sl] = (y * (g * jax.nn.sigmoid(g))).astype(BF16)
        for h in range(DIFF_HEADS):
            sl = slice(h * DIFF_HW, (h + 1) * DIFF_HW)
            o = do_ref[:, sl]
            y = o * lax.rsqrt(jnp.mean(o * o, axis=-1, keepdims=True) + EPS) * gd_ref[...] * (1.0 - LAMBDA_INIT)
            dy_ref[:, sl] = y.astype(BF16)

    pa = _dot(ry_ref[...], wa_ref[...].astype(BF16))
    pb = _dot(dy_ref[...], wb_ref[...].astype(BF16))
    m = jax.nn.sigmoid(ga_ref[...]) * pa + jax.nn.sigmoid(gb_ref[...]) * pb
    m_ref[...] = m.astype(m_ref.dtype)


def _mixer_gate(z2, ret_o, diff_o, ret_norm_g, diff_norm_g, wa, wb, *, bm=512, bn=512):
    m = z2.shape[0]
    bm = min(bm, m)
    return pl.pallas_call(
        _mix_kernel,
        grid=(m // bm, D_MODEL // bn),
        in_specs=[
            pl.BlockSpec((bm, RET_W), lambda i, j: (i, 0)),
            pl.BlockSpec((bm, RET_W), lambda i, j: (i, OFF_RG // RET_W)),
            pl.BlockSpec((bm, DIFF_W), lambda i, j: (i, 0)),
            pl.BlockSpec((bm, bn), lambda i, j: (i, OFF_GA // bn + j)),
            pl.BlockSpec((bm, bn), lambda i, j: (i, OFF_GB // bn + j)),
            pl.BlockSpec((1, RET_W), lambda i, j: (0, 0)),
            pl.BlockSpec((1, DIFF_HW), lambda i, j: (0, 0)),
            pl.BlockSpec((RET_W, bn), lambda i, j: (0, j)),
            pl.BlockSpec((DIFF_W, bn), lambda i, j: (0, j))],
        out_specs=pl.BlockSpec((bm, bn), lambda i, j: (i, j)),
        out_shape=jax.ShapeDtypeStruct((m, D_MODEL), BF16),
        scratch_shapes=[pltpu.VMEM((bm, RET_W), BF16), pltpu.VMEM((bm, DIFF_W), BF16)],
        compiler_params=_cparams("parallel", "arbitrary"),
        name="mixer_gate",
    )(ret_o, z2, diff_o, z2, z2, ret_norm_g.reshape(1, RET_W), diff_norm_g.reshape(1, DIFF_HW), wa, wb)


def _xattn_kernel(q_ref, k_ref, v_ref, o_ref):
    scale = X_DH ** -0.5
    q = q_ref[0]
    k = k_ref[0]
    v = v_ref[0]
    for h in range(X_HEADS):
        sl = slice(h * X_DH, (h + 1) * X_DH)
        s = _dot_nt(q[:, sl].astype(BF16), k[:, sl].astype(BF16)) * scale
        s = s - jnp.max(s, axis=-1, keepdims=True)
        p = jnp.exp(s)
        p = p / jnp.sum(p, axis=-1, keepdims=True)
        o_ref[0, :, sl] = _dot(p.astype(BF16), v[:, sl].astype(BF16)).astype(o_ref.dtype)


def _cross_attention(q3, mem_k, mem_v, *, tq=512):
    b, s, w = q3.shape
    tq = min(tq, s)
    mk = mem_k.reshape(b, MEM_LEN, w)
    mv = mem_v.reshape(b, MEM_LEN, w)
    return pl.pallas_call(
        _xattn_kernel,
        grid=(b, s // tq),
        in_specs=[pl.BlockSpec((1, tq, w), lambda bi, i: (bi, i, 0)),
                  pl.BlockSpec((1, MEM_LEN, w), lambda bi, i: (bi, 0, 0)),
                  pl.BlockSpec((1, MEM_LEN, w), lambda bi, i: (bi, 0, 0))],
        out_specs=pl.BlockSpec((1, tq, w), lambda bi, i: (bi, i, 0)),
        out_shape=jax.ShapeDtypeStruct((b, s, w), BF16),
        compiler_params=_cparams("parallel", "parallel"),
        name="cross_attention",
    )(q3, mk, mv)


def _topk_rows(s_ref, work_ref, rank_ref, sorted_ref, *, break_ties):
    shape = s_ref.shape
    nk = shape[0]
    iota = lax.broadcasted_iota(jnp.int32, shape, 0).astype(F32)
    work_ref[...] = s_ref[...]
    rank_ref[...] = jnp.full(shape, float(PEER_TOPK), F32)
    for r in range(PEER_TOPK):
        w = work_ref[...]
        m = jnp.max(w, axis=0, keepdims=True)
        sel = w == m
        if break_ties:
            sel = iota == jnp.min(jnp.where(sel, iota, float(nk)), axis=0, keepdims=True)
        rank_ref[...] = jnp.where(sel, float(r), rank_ref[...])
        work_ref[...] = jnp.where(sel, NEG_INF, w)
        sorted_ref[r:r + 1, :] = m


def _topk_exact(s_ref, work_ref, rank_ref, sorted_ref):
    _topk_rows(s_ref, work_ref, rank_ref, sorted_ref, break_ties=False)
    taken = jnp.sum(jnp.where(rank_ref[...] < PEER_TOPK, 1.0, 0.0), axis=0, keepdims=True)
    tied = jnp.max(jnp.where(taken == float(PEER_TOPK), 0.0, 1.0)) > 0.0

    @pl.when(tied)
    def _():
        _topk_rows(s_ref, work_ref, rank_ref, sorted_ref, break_ties=True)


def _route_kernel(q_ref, keys_ref, e1_ref, n1_ref, e2_ref, r2_ref,
                  st_ref, work_ref, rank1_ref, rank2_ref, s1s_ref, s2s_ref):
    q = q_ref[...]
    for c in range(2):
        qc = q[:, c * PEER_HALF:(c + 1) * PEER_HALF].astype(BF16)
        st_ref[c] = _dot_nt(keys_ref[c, 0].astype(BF16), qc)
    _topk_exact(st_ref.at[0], work_ref, rank1_ref, s1s_ref)
    _topk_exact(st_ref.at[1], work_ref, rank2_ref, s2s_ref)
    st = (st_ref[0], st_ref[1])
    s1 = s1s_ref[...]
    s2 = s2s_ref[...]

    iota = lax.broadcasted_iota(jnp.int32, s1.shape, 0).astype(F32)
    ptr = jnp.zeros(s1.shape, F32)
    count = jnp.zeros(s1.shape, F32)
    front = s1 + s2[0:1, :]
    zsum = jnp.zeros((1, s1.shape[1]), F32)
    top0 = None
    for kk in range(PEER_TOPK):
        m = jnp.max(front, axis=0, keepdims=True)
        if kk == 0:
            top0 = m
        zsum = zsum + jnp.exp(m - top0)
        istar = jnp.min(jnp.where(front == m, iota, float(PEER_TOPK)), axis=0, keepdims=True)
        oh = iota == istar
        count = count + jnp.where(oh, 1.0, 0.0)
        pnew = jnp.sum(jnp.where(oh, ptr, 0.0), axis=0, keepdims=True) + 1.0
        ptr = jnp.where(oh, pnew, ptr)
        s2n = jnp.max(jnp.where(iota == pnew, s2, NEG_INF), axis=0, keepdims=True)
        s1sel = jnp.max(jnp.where(oh, s1, NEG_INF), axis=0, keepdims=True)
        front = jnp.where(oh, s1sel + s2n, front)

    rank1 = rank1_ref[...]
    rank2 = rank2_ref[...]
    n1 = jnp.zeros(rank1.shape, F32)
    for r in range(PEER_TOPK):
        n1 = jnp.where(rank1 == float(r), count[r:r + 1, :], n1)
    inv_z = 1.0 / zsum
    e1_ref[0] = jnp.where(rank1 < PEER_TOPK, jnp.exp(st[0] - s1[0:1, :]) * inv_z, 0.0)
    e2_ref[0] = jnp.where(rank2 < PEER_TOPK, jnp.exp(st[1] - s2[0:1, :]), 0.0).astype(e2_ref.dtype)
    n1_ref[0] = n1
    r2_ref[0] = rank2.astype(r2_ref.dtype)


def _peer_route(qp, peer_keys, *, tt=512):
    t = qp.shape[0]
    tt = min(tt, t)
    hw = 2 * PEER_HALF
    wide = jax.ShapeDtypeStruct((PEER_HEADS, PEER_NKEYS, t), F32)
    narrow = jax.ShapeDtypeStruct((PEER_HEADS, PEER_NKEYS, t), BF16)
    ospec = pl.BlockSpec((1, PEER_NKEYS, tt), lambda i, h: (h, 0, i))
    return pl.pallas_call(
        _route_kernel,
        grid=(t // tt, PEER_HEADS),
        in_specs=[pl.BlockSpec((tt, hw), lambda i, h: (i, h)),
                  pl.BlockSpec((2, 1, PEER_NKEYS, PEER_HALF), lambda i, h: (0, h, 0, 0))],
        out_specs=[ospec, ospec, ospec, ospec],
        out_shape=[wide, wide, narrow, narrow],
        scratch_shapes=[pltpu.VMEM((2, PEER_NKEYS, tt), F32),
                        pltpu.VMEM((PEER_NKEYS, tt), F32), pltpu.VMEM((PEER_NKEYS, tt), F32),
                        pltpu.VMEM((PEER_NKEYS, tt), F32),
                        pltpu.VMEM((PEER_TOPK, tt), F32), pltpu.VMEM((PEER_TOPK, tt), F32)],
        compiler_params=_cparams("parallel", "parallel"),
        name="peer_route",
    )(qp, peer_keys)


def _transpose_tile_kernel(x_ref, o_ref):
    o_ref[0] = x_ref[...].T.astype(o_ref.dtype)


def _transposed_tiles(x, rows):
    n, d = x.shape
    return pl.pallas_call(
        _transpose_tile_kernel,
        grid=(n // rows,),
        in_specs=[pl.BlockSpec((rows, d), lambda i: (i, 0))],
        out_specs=pl.BlockSpec((1, d, rows), lambda i: (i, 0, 0)),
        out_shape=jax.ShapeDtypeStruct((n // rows, d, rows), BF16),
        compiler_params=_cparams("parallel"),
        name="transpose_tiles",
    )(x)


def _peer_dense_kernel(h2_ref, gf_ref, gl_ref, u_ref, vt_ref, e1_ref, n1_ref, e2_ref, r2_ref, y_ref,
                       xn_ref, acc_ref, ht0_ref, ht1_ref, *, et, n_tiles, th):
    s = pl.program_id(1)
    slabs = et // PEER_NKEYS
    tt = xn_ref.shape[0]
    ht_refs = (ht0_ref, ht1_ref)

    def project(par):
        ht_refs[par][...] = _dot_nt(u_ref[...], xn_ref[...])

    def finish(par):
        for c in range(tt // th):
            cols = slice(c * th, (c + 1) * th)
            act = jax.nn.gelu(ht_refs[par][:, cols]).astype(BF16)
            ws = []
            for sl in range(slabs):
                a = (s - 1) * slabs + sl
                w = None
                for h in range(PEER_HEADS):
                    e1row = e1_ref[h, pl.ds(a, 1), cols].astype(BF16)
                    n1row = n1_ref[h, pl.ds(a, 1), cols].astype(BF16)
                    term = jnp.where(r2_ref[h, :, cols] < n1row, e2_ref[h, :, cols], 0.0) * e1row
                    w = term if w is None else w + term
                ws.append(w)
            wt = ws[0] if slabs == 1 else jnp.concatenate(ws, axis=0)
            acc_ref[:, cols] += _dot(vt_ref[0], act * wt)

    @pl.when(s == 0)
    def _():
        x = h2_ref[...]
        ms = jnp.mean(x * x, axis=-1, keepdims=True)
        xn_ref[...] = (x * lax.rsqrt(ms + EPS) * gf_ref[...]).astype(BF16)
        acc_ref[...] = jnp.zeros_like(acc_ref)
        project(0)

    for par in (0, 1):
        @pl.when(jnp.logical_and(jnp.logical_and(s > 0, s < n_tiles), lax.rem(s, 2) == par))
        def _():
            project(par)
            finish(1 - par)

    @pl.when(s == n_tiles)
    def _():
        finish((n_tiles - 1) % 2)
        r = h2_ref[...] + acc_ref[...].T
        ms = jnp.mean(r * r, axis=-1, keepdims=True)
        y_ref[...] = r * lax.rsqrt(ms + EPS) * gl_ref[...]


def _peer_dense(h2, g_ffn, g_final, u, vt, route, *, tt=512, th=256):
    t = h2.shape[0]
    tt = min(tt, t)
    n_tiles, _, et = vt.shape
    rspec = pl.BlockSpec((PEER_HEADS, PEER_NKEYS, tt), lambda i, s: (0, 0, i))
    return pl.pallas_call(
        functools.partial(_peer_dense_kernel, et=et, n_tiles=n_tiles, th=min(th, tt)),
        grid=(t // tt, n_tiles + 1),
        in_specs=[pl.BlockSpec((tt, D_MODEL), lambda i, s: (i, 0)),
                  pl.BlockSpec((1, D_MODEL), lambda i, s: (0, 0)),
                  pl.BlockSpec((1, D_MODEL), lambda i, s: (0, 0)),
                  pl.BlockSpec((et, D_MODEL), lambda i, s: (jnp.minimum(s, n_tiles - 1), 0)),
                  pl.BlockSpec((1, D_MODEL, et), lambda i, s: (jnp.maximum(s - 1, 0), 0, 0)),
                  rspec, rspec, rspec, rspec],
        out_specs=pl.BlockSpec((tt, D_MODEL), lambda i, s: (i, 0)),
        out_shape=jax.ShapeDtypeStruct((t, D_MODEL), F32),
        scratch_shapes=[pltpu.VMEM((tt, D_MODEL), BF16), pltpu.VMEM((D_MODEL, tt), F32),
                        pltpu.VMEM((et, tt), F32), pltpu.VMEM((et, tt), F32)],
        compiler_params=_cparams("parallel", "arbitrary", vmem=PEER_VMEM_LIMIT),
        name="peer_dense",
    )(h2, g_ffn.reshape(1, D_MODEL), g_final.reshape(1, D_MODEL), u, vt, *route)


def _tail(h1, mem_k, mem_v, w, batch):
    t = h1.shape[0]
    qx = _matmul(h1, w["xq"], gain=w["g_cross"], name="xq_proj")
    ox = _cross_attention(qx.reshape(batch, t // batch, X_HEADS * X_DH), mem_k, mem_v)
    h2 = _matmul(ox.reshape(t, X_HEADS * X_DH), w["xo"], residual=h1, name="xo_proj")
    qp = _matmul(h2, w["pq"], gain=w["g_ffn"], name="peer_query")
    route = _peer_route(qp, w["peer_keys"])
    return _peer_dense(h2, w["g_ffn"], w["g_final"], w["peer_u"], w["peer_vt"], route)


def _mixer_out(x2, z2, ret_o, diff_o, w):
    m = _mixer_gate(z2, ret_o, diff_o, w["ret_norm_g"], w["diff_norm_g"], w["branch_a"], w["branch_b"])
    return _matmul(m, w["out"], residual=x2, name="out_proj")


def kernel(x_prompt, x_sample, cache_k, cache_v, state_ret, cache_mem_k, cache_mem_v, page_table, mem_prompt, g_mix, w_in, ret_norm_g, diff_norm_g, lambda_q1, lambda_k1, lambda_q2, lambda_k2, w_branch_a, w_branch_b, w_out, g_cross, w_xq, w_mem_kv, w_xo, g_ffn, w_pq, peer_keys, peer_u, peer_v, g_final):
    bp, sp, d = x_prompt.shape
    bd, td, _ = x_sample.shape
    w = dict(
        ret_norm_g=ret_norm_g, diff_norm_g=diff_norm_g, g_cross=g_cross, g_ffn=g_ffn, g_final=g_final,
        branch_a=w_branch_a, branch_b=w_branch_b, out=w_out, xq=w_xq, xo=w_xo, pq=w_pq, peer_keys=peer_keys,
        peer_u=peer_u.astype(BF16), peer_vt=_transposed_tiles(peer_v, PEER_EXPERT_TILE))
    w_in_b = w_in
    lams = [a.reshape(1, DIFF_DH).astype(F32) for a in (lambda_q1, lambda_k1, lambda_q2, lambda_k2)]

    xp2 = x_prompt.reshape(bp * sp, d)
    z2 = _matmul(xp2, w_in_b, gain=g_mix, name="in_proj")
    z3 = z2.reshape(bp, sp, IN_W)
    ret_o, ret_state_prompt = _retention(z3, jnp.arange(sp, dtype=jnp.int32), None, RET_CHUNK, "retention_prompt")
    diff_o = _diff_prompt(z3, lams)
    h1 = _mixer_out(xp2, z2, ret_o.reshape(bp * sp, RET_W), diff_o.reshape(bp * sp, DIFF_W), w)
    mem_kv = _matmul(mem_prompt.reshape(bp * MEM_LEN, d), w_mem_kv, name="mem_kv_proj")
    mem_kv = mem_kv.reshape(bp, MEM_LEN, 2, X_HEADS, X_DH)
    mem_k_prompt, mem_v_prompt = mem_kv[:, :, 0], mem_kv[:, :, 1]
    y_prompt = _tail(h1, mem_k_prompt, mem_v_prompt, w, bp).reshape(bp, sp, d)
    k_prompt = z3[:, :, OFF_DK:OFF_DK + DIFF_W].reshape(bp, sp, DIFF_HEADS, DIFF_HW)
    v_prompt = z3[:, :, OFF_DV:OFF_DV + DIFF_W].reshape(bp, sp, DIFF_HEADS, DIFF_HW)

    past = page_table.shape[1] * PAGE_SIZE
    xs2 = x_sample.reshape(bd * td, d)
    zs2 = _matmul(xs2, w_in_b, gain=g_mix, name="in_proj")
    zs3 = zs2.reshape(bd, td, IN_W)
    ret_o_s, ret_state_sample = _retention(zs3, past + jnp.arange(td, dtype=jnp.int32), state_ret, td,
                                           "retention_sample")
    diff_o_s = _diff_sample(zs3, cache_k, cache_v, page_table, lams)
    h1s = _mixer_out(xs2, zs2, ret_o_s.reshape(bd * td, RET_W), diff_o_s.reshape(bd * td, DIFF_W), w)
    y_sample = _tail(h1s, cache_mem_k, cache_mem_v, w, bd).reshape(bd, td, d)
    k_sample = zs3[:, :, OFF_DK:OFF_DK + DIFF_W].reshape(bd, td, DIFF_HEADS, DIFF_HW)
    v_sample = zs3[:, :, OFF_DV:OFF_DV + DIFF_W].reshape(bd, td, DIFF_HEADS, DIFF_HW)

    return (y_prompt, y_sample, k_prompt, v_prompt, ret_state_prompt, mem_k_prompt, mem_v_prompt,
            k_sample, v_sample, ret_state_sample)
```

```python
import functools
import math

import numpy as np
import jax
import jax.numpy as jnp
from jax import lax
from jax.experimental import pallas as pl
from jax.experimental.pallas import tpu as pltpu

F32 = jnp.float32
BF16 = jnp.bfloat16

D_MODEL = 2048
RET_HEADS = 8
RET_DK = 128
RET_DV = 128
RET_CHUNK = 128
DIFF_HEADS = 4
DIFF_DH = 128
DIFF_HW = 2 * DIFF_DH
PAGE_SIZE = 128
MEM_LEN = 256
X_HEADS = 4
X_DH = 128
PEER_HEADS = 8
PEER_NKEYS = 128
PEER_HALF = 128
PEER_TOPK = 16
ROPE_BASE = 10000.0
LAMBDA_INIT = 0.8 - 0.6 * math.exp(-0.3 * 0)
EPS = 1e-6

RET_W = RET_HEADS * RET_DK
DIFF_W = DIFF_HEADS * DIFF_HW
OFF_RQ, OFF_RK, OFF_RV, OFF_RG = 0, RET_W, 2 * RET_W, 3 * RET_W
OFF_DQ = 4 * RET_W
OFF_DK = OFF_DQ + DIFF_W
OFF_DV = OFF_DK + DIFF_W
OFF_GA = OFF_DV + DIFF_W
OFF_GB = OFF_GA + D_MODEL
IN_W = OFF_GB + D_MODEL

LANES = 128
SUBLANES = 8
VMEM_LIMIT = 48 * 1024 * 1024
PEER_VMEM_LIMIT = 58 * 1024 * 1024
PEER_EXPERT_TILE = 512
SAMPLE_PAGES_PER_STEP = 16

NEG_INF = float("-inf")


def _cparams(*sem, vmem=VMEM_LIMIT, flags=None):
    return pltpu.CompilerParams(dimension_semantics=sem, vmem_limit_bytes=vmem, flags=flags)


def _dot(a, b):
    return jnp.dot(a, b, preferred_element_type=F32)


def _dot_nt(a, b):
    return lax.dot_general(a, b, (((1,), (1,)), ((), ())), preferred_element_type=F32)


def _dot_tn(a, b):
    return lax.dot_general(a, b, (((0,), (0,)), ((), ())), preferred_element_type=F32)


def _mm_kernel(*refs, norm, residual, head_copies):
    it = iter(refs)
    x_ref = next(it)
    g_ref = next(it) if norm else None
    w_ref = next(it)
    r_ref = next(it) if residual else None
    o_ref = next(it)
    copy_refs = [next(it) for _ in head_copies]
    xs_ref = next(it)
    j = pl.program_id(1)

    @pl.when(j == 0)
    def _():
        x = x_ref[...].astype(F32)
        if norm:
            ms = jnp.mean(x * x, axis=-1, keepdims=True)
            x = x * lax.rsqrt(ms + EPS) * g_ref[...]
        xs_ref[...] = x.astype(BF16)

    acc = _dot(xs_ref[...], w_ref[...])
    if residual:
        acc = r_ref[...] + acc
    o_ref[...] = acc.astype(o_ref.dtype)
    chunks = acc.shape[1] // LANES
    for (first, count), c_ref in zip(head_copies, copy_refs):
        for q in range(count):
            @pl.when(j == first + q)
            def _():
                for p in range(chunks):
                    head, half = divmod(q * chunks + p, 2)
                    c_ref[:, half * DIFF_HEADS + head, :] = acc[:, p * LANES:(p + 1) * LANES]


def _matmul(x, w, *, name, gain=None, residual=None, out_dtype=F32, bm=1024, bn=512, head_major=()):
    m, k = x.shape
    n = w.shape[1]
    bm = min(bm, m)
    bn = min(bn, n)
    assert m % bm == 0 and n % bn == 0
    norm = gain is not None
    res = residual is not None
    head_copies = []
    out_specs = [pl.BlockSpec((bm, bn), lambda i, j: (i, j))]
    out_shape = [jax.ShapeDtypeStruct((m, n), out_dtype)]
    for off in head_major:
        assert off % bn == 0 and DIFF_W % bn == 0 and bn % DIFF_HW == 0
        head_copies.append((off // bn, DIFF_W // bn))
        out_specs.append(pl.BlockSpec((bm, 2 * DIFF_HEADS, DIFF_DH), lambda i, j: (i, 0, 0)))
        out_shape.append(jax.ShapeDtypeStruct((m, 2 * DIFF_HEADS, DIFF_DH), F32))
    in_specs = [pl.BlockSpec((bm, k), lambda i, j: (i, 0))]
    args = [x]
    if norm:
        in_specs.append(pl.BlockSpec((1, k), lambda i, j: (0, 0)))
        args.append(gain.reshape(1, k).astype(F32))
    in_specs.append(pl.BlockSpec((k, bn), lambda i, j: (0, j)))
    args.append(w)
    if res:
        in_specs.append(pl.BlockSpec((bm, bn), lambda i, j: (i, j)))
        args.append(residual)
    outs = pl.pallas_call(
        functools.partial(_mm_kernel, norm=norm, residual=res, head_copies=tuple(head_copies)),
        grid=(m // bm, n // bn),
        in_specs=in_specs,
        out_specs=out_specs,
        out_shape=out_shape,
        scratch_shapes=[pltpu.VMEM((bm, k), BF16)],
        compiler_params=_cparams("parallel", "arbitrary"),
        name=name,
    )(*args)
    return outs if head_major else outs[0]


def _heads_view(x, batch):
    s = x.shape[0] // batch
    x5 = x.reshape(batch, s, 2, DIFF_HEADS, DIFF_DH)
    return jnp.transpose(x5, (0, 1, 3, 2, 4)).reshape(batch, s, DIFF_HEADS, DIFF_HW)


def _ret_tables(chunk_len):
    c = RET_CHUNK
    lg = jnp.log1p(-jnp.exp2(-5.0 - jnp.arange(RET_HEADS, dtype=F32)))
    i = jnp.arange(c, dtype=F32)
    rel = i[:, None] - i[None, :]
    causal = rel >= 0
    dmat = jnp.where(causal[None], jnp.exp(jnp.where(causal, rel, 0.0)[None] * lg[:, None, None]), 0.0)
    dec_in = jnp.exp((i + 1.0)[None, :] * lg[:, None])
    dec_out = jnp.exp((chunk_len - 1.0 - i)[None, :] * lg[:, None])
    dec_chunk = jnp.exp(chunk_len * lg)
    ones = jnp.ones((RET_HEADS, c, LANES), F32)
    return (dmat, dec_in[:, :, None] * ones, dec_out[:, :, None] * ones,
            dec_chunk[:, None, None] * ones)


def _rope_tables(pos):
    half = RET_DK // 2
    freqs = jnp.exp(-math.log(ROPE_BASE) * jnp.arange(half, dtype=F32) / half)
    ang = pos.astype(F32)[:, None] * freqs[None, :]
    cos, sin = jnp.cos(ang), jnp.sin(ang)
    return jnp.concatenate([cos, cos], axis=-1), jnp.concatenate([-sin, sin], axis=-1)


def _ret_kernel(*refs, rows, has_state):
    it = iter(refs)
    q_ref, k_ref, v_ref, cos_ref, sin_ref = (next(it) for _ in range(5))
    dmat_ref, din_ref, dout_ref, dch_ref = (next(it) for _ in range(4))
    s0_ref = next(it) if has_state else None
    o_ref, sfin_ref, s_ref = next(it), next(it), next(it)
    pad_ref = next(it) if rows < RET_CHUNK else None
    n = pl.program_id(1)

    @pl.when(n == 0)
    def _():
        if has_state:
            s_ref[...] = s0_ref[0]
        else:
            s_ref[...] = jnp.zeros_like(s_ref)

    def full(x):
        if rows == RET_CHUNK:
            return x
        width = x.shape[1]
        pad_ref[:, 0:width] = jnp.zeros((RET_CHUNK, width), F32)
        pad_ref[0:rows, 0:width] = x
        return pad_ref[:, 0:width]

    def rot(x, cos, sin):
        return x * cos + pltpu.roll(x, RET_DK // 2, axis=1) * sin

    cos, sin = full(cos_ref[...]), full(sin_ref[...])
    q_all, k_all, v_all = full(q_ref[0]), full(k_ref[0]), full(v_ref[0])
    for h in range(RET_HEADS):
        sl = slice(h * RET_DK, (h + 1) * RET_DK)
        q = rot(q_all[:, sl], cos, sin)
        k = rot(k_all[:, sl], cos, sin) * (RET_DK ** -0.5)
        v = v_all[:, sl].astype(BF16)
        s_prev = s_ref[h]
        qb = q.astype(BF16)
        scores = _dot_nt(qb, k.astype(BF16)) * dmat_ref[h]
        out = _dot(scores.astype(BF16), v) + _dot(qb, s_prev.astype(BF16)) * din_ref[h]
        s_ref[h] = dch_ref[h] * s_prev + _dot_tn((k * dout_ref[h]).astype(BF16), v)
        o_ref[0, :, sl] = out if rows == RET_CHUNK else out[0:rows, :]

    @pl.when(n == pl.num_programs(1) - 1)
    def _():
        sfin_ref[0] = s_ref[...]


def _retention(z3, pos, state0, chunk_len, name):
    b, s, _ = z3.shape
    rows = min(s, RET_CHUNK)
    nchunks = s // rows
    cos, sin = _rope_tables(pos)
    dmat, din, dout, dch = _ret_tables(float(chunk_len))
    has_state = state0 is not None
    qkv_spec = lambda off: pl.BlockSpec((1, rows, RET_W), lambda bi, n: (bi, n, off // RET_W))
    tab_spec = pl.BlockSpec((RET_HEADS, RET_CHUNK, LANES), lambda bi, n: (0, 0, 0))
    state_spec = pl.BlockSpec((1, RET_HEADS, RET_DK, RET_DV), lambda bi, n: (bi, 0, 0, 0))
    in_specs = [qkv_spec(OFF_RQ), qkv_spec(OFF_RK), qkv_spec(OFF_RV),
                pl.BlockSpec((rows, LANES), lambda bi, n: (n, 0)),
                pl.BlockSpec((rows, LANES), lambda bi, n: (n, 0)),
                tab_spec, tab_spec, tab_spec, tab_spec]
    args = [z3, z3, z3, cos, sin, dmat, din, dout, dch]
    if has_state:
        in_specs.append(state_spec)
        args.append(state0)
    scratch = [pltpu.VMEM((RET_HEADS, RET_DK, RET_DV), F32)]
    if rows < RET_CHUNK:
        scratch.append(pltpu.VMEM((RET_CHUNK, RET_W), F32))
    return pl.pallas_call(
        functools.partial(_ret_kernel, rows=rows, has_state=has_state),
        grid=(b, nchunks),
        in_specs=in_specs,
        out_specs=[pl.BlockSpec((1, rows, RET_W), lambda bi, n: (bi, n, 0)), state_spec],
        out_shape=[jax.ShapeDtypeStruct((b, s, RET_W), F32),
                   jax.ShapeDtypeStruct((b, RET_HEADS, RET_DK, RET_DV), F32)],
        scratch_shapes=scratch,
        compiler_params=_cparams("parallel", "arbitrary"),
        name=name,
    )(*args)


def _lambda_value(lq1_ref, lk1_ref, lq2_ref, lk2_ref):
    a = jnp.sum(lq1_ref[...] * lk1_ref[...], axis=-1, keepdims=True)
    b = jnp.sum(lq2_ref[...] * lk2_ref[...], axis=-1, keepdims=True)
    return jnp.exp(a) - jnp.exp(b) + LAMBDA_INIT


def _online_update(s, pv, m_ref, l_ref, acc_ref, idx):
    def lanes(x, width):
        return x if width == LANES else jnp.concatenate([x] * (width // LANES), axis=1)

    m_prev = m_ref[idx]
    m_new = jnp.maximum(m_prev, jnp.max(s, axis=-1, keepdims=True))
    alpha = jnp.exp(m_prev - m_new)
    p = jnp.exp(s - lanes(m_new, s.shape[1]))
    l_ref[idx] = alpha * l_ref[idx] + jnp.sum(p, axis=-1, keepdims=True)
    acc = acc_ref[idx]
    acc_ref[idx] = lanes(alpha, acc.shape[1]) * acc + pv(p)
    m_ref[idx] = m_new


def _diff_prompt_kernel(qi_ref, ki_ref, q_ref, k_ref, v_ref, lq1, lk1, lq2, lk2, o_ref,
                        m_ref, l_ref, acc_ref):
    pair = pl.program_id(2)
    qi, ki = qi_ref[pair], ki_ref[pair]
    scale = DIFF_DH ** -0.5

    @pl.when(ki == 0)
    def _():
        m_ref[...] = jnp.full_like(m_ref, NEG_INF)
        l_ref[...] = jnp.zeros_like(l_ref)
        acc_ref[...] = jnp.zeros_like(acc_ref)

    def step(masked):
        q = q_ref[0]
        k = k_ref[0]
        v = v_ref[0].astype(BF16)
        for c in range(2):
            sl = slice(c * DIFF_DH, (c + 1) * DIFF_DH)
            s = _dot_nt(q[:, sl].astype(BF16), k[:, sl].astype(BF16)) * scale
            if masked:
                row = lax.broadcasted_iota(jnp.int32, s.shape, 0)
                col = lax.broadcasted_iota(jnp.int32, s.shape, 1)
                s = jnp.where(row >= col, s, NEG_INF)
            _online_update(s, lambda p: _dot(p.astype(BF16), v), m_ref, l_ref, acc_ref, c)

    @pl.when(ki < qi)
    def _():
        step(False)

    @pl.when(ki == qi)
    def _():
        step(True)
        lam = _lambda_value(lq1, lk1, lq2, lk2)
        o0 = acc_ref[0] / l_ref[0][:, 0:1]
        o1 = acc_ref[1] / l_ref[1][:, 0:1]
        o_ref[0] = o0 - lam * o1


def _diff_prompt(z3, lams, *, tq=512):
    b, s, _ = z3.shape
    nq = s // tq
    pairs = [(qi, ki) for qi in range(nq) for ki in range(qi + 1)]
    qi_tab = jnp.asarray([p[0] for p in pairs], jnp.int32)
    ki_tab = jnp.asarray([p[1] for p in pairs], jnp.int32)
    col = lambda off: off // DIFF_HW
    lam_spec = pl.BlockSpec((1, DIFF_DH), lambda bi, h, p, qt, kt: (0, 0))
    grid_spec = pltpu.PrefetchScalarGridSpec(
        num_scalar_prefetch=2,
        grid=(b, DIFF_HEADS, len(pairs)),
        in_specs=[
            pl.BlockSpec((1, tq, DIFF_HW), lambda bi, h, p, qt, kt: (bi, qt[p], col(OFF_DQ) + h)),
            pl.BlockSpec((1, tq, DIFF_HW), lambda bi, h, p, qt, kt: (bi, kt[p], col(OFF_DK) + h)),
            pl.BlockSpec((1, tq, DIFF_HW), lambda bi, h, p, qt, kt: (bi, kt[p], col(OFF_DV) + h)),
            lam_spec, lam_spec, lam_spec, lam_spec],
        out_specs=pl.BlockSpec((1, tq, DIFF_HW), lambda bi, h, p, qt, kt: (bi, qt[p], h)),
        scratch_shapes=[pltpu.VMEM((2, tq, LANES), F32), pltpu.VMEM((2, tq, LANES), F32),
                        pltpu.VMEM((2, tq, DIFF_HW), F32)])
    return pl.pallas_call(
        _diff_prompt_kernel,
        grid_spec=grid_spec,
        out_shape=jax.ShapeDtypeStruct((b, s, DIFF_W), F32),
        compiler_params=_cparams("parallel", "parallel", "arbitrary"),
        name="diff_prompt",
    )(qi_tab, ki_tab, z3, z3, z3, *lams)


def _diff_sample_kernel(pt_ref, q_ref, kn_ref, vn_ref, *rest, t, g):
    k_refs, v_refs = rest[:g], rest[g:2 * g]
    lq1, lk1, lq2, lk2, o_ref, qm_ref, m_ref, l_ref, acc_ref, pad_ref = rest[2 * g:]
    p = pl.program_id(1)
    scale = DIFF_DH ** -0.5
    hrows = 2 * t
    nrow = DIFF_HEADS * hrows
    slots = 2 * DIFF_HEADS
    prow = PAGE_SIZE * slots
    slot_bits, hrow_bits = slots.bit_length() - 1, hrows.bit_length() - 1

    @pl.when(p == 0)
    def _():
        m_ref[...] = jnp.full_like(m_ref, NEG_INF)
        l_ref[...] = jnp.zeros_like(l_ref)
        acc_ref[...] = jnp.zeros_like(acc_ref)
        q = q_ref[0]
        for h in range(DIFF_HEADS):
            for c in range(2):
                r0 = h * hrows + c * t
                lo = h * DIFF_HW + c * DIFF_DH
                qm_ref[r0:r0 + t, :] = q[:, lo:lo + DIFF_DH]

    qm = qm_ref[...].astype(BF16)
    row = lax.broadcasted_iota(jnp.int32, (nrow, prow), 0)
    col = lax.broadcasted_iota(jnp.int32, (nrow, prow), 1)
    row_half = jnp.where(jnp.bitwise_and(row, hrows - 1) >= t, DIFF_HEADS, 0)
    row_slot = row_half + lax.shift_right_logical(row, hrow_bits)
    match = jnp.bitwise_and(col, slots - 1) == row_slot

    def page_scores(kpage):
        s = _dot_nt(qm, kpage.astype(BF16)) * scale
        return jnp.where(match, s, NEG_INF)

    def page_values(vpage):
        v3 = vpage.reshape(PAGE_SIZE, slots, DIFF_DH)
        other = pltpu.roll(v3, DIFF_HEADS, 1).reshape(prow, DIFF_DH)
        return jnp.concatenate([vpage.astype(BF16), other.astype(BF16)], axis=1)

    def update(scores, values):
        def pv(pr):
            out = None
            for i, vm in enumerate(values):
                o = _dot(pr[:, i * prow:(i + 1) * prow].astype(BF16), vm)
                out = o if out is None else out + o
            return out
        s = scores[0] if len(scores) == 1 else jnp.concatenate(scores, axis=1)
        _online_update(s, pv, m_ref, l_ref, acc_ref, slice(None))

    update([page_scores(k[0]) for k in k_refs], [page_values(v[0]) for v in v_refs])

    @pl.when(p == pl.num_programs(1) - 1)
    def _():
        pad_ref[...] = jnp.zeros_like(pad_ref)
        kn, vn = kn_ref[0], vn_ref[0]
        for tok in range(t):
            for c in range(2):
                for h in range(DIFF_HEADS):
                    r = tok * slots + c * DIFF_HEADS + h
                    lo = h * DIFF_HW + c * DIFF_DH
                    pad_ref[0, r:r + 1, :] = kn[tok:tok + 1, lo:lo + DIFF_DH]
                    pad_ref[1, r:r + 1, :] = vn[tok:tok + 1, lo:lo + DIFF_DH]
        s = page_scores(pad_ref[0])
        causal = lax.shift_right_logical(col, slot_bits) <= jnp.bitwise_and(row, t - 1)
        update([jnp.where(causal, s, NEG_INF)], [page_values(pad_ref[1])])
        lam = _lambda_value(lq1, lk1, lq2, lk2)
        acc = acc_ref[...]
        swapped = jnp.concatenate([acc[:, DIFF_DH:], acc[:, :DIFF_DH]], axis=1)
        out_row = lax.broadcasted_iota(jnp.int32, acc.shape, 0)
        second_half = jnp.bitwise_and(out_row, hrows - 1) >= t
        o = jnp.where(second_half, swapped, acc) / l_ref[...][:, 0:1]
        for h in range(DIFF_HEADS):
            r0 = h * hrows
            o_ref[0, :, h * DIFF_HW:(h + 1) * DIFF_HW] = o[r0:r0 + t] - lam * o[r0 + t:r0 + hrows]


def _page_rows(cache):
    n_pool = cache.shape[0]
    c5 = cache.reshape(n_pool, PAGE_SIZE, DIFF_HEADS, 2, DIFF_DH)
    return jnp.transpose(c5, (0, 1, 3, 2, 4)).reshape(n_pool, PAGE_SIZE * 2 * DIFF_HEADS, DIFF_DH)


def _diff_sample(zs3, cache_k, cache_v, page_table, lams):
    b, t, _ = zs3.shape
    assert t & (t - 1) == 0
    n_pages = page_table.shape[1]
    g = math.gcd(SAMPLE_PAGES_PER_STEP, n_pages)
    nrow = DIFF_HEADS * 2 * t
    prow = PAGE_SIZE * 2 * DIFF_HEADS
    zcol = lambda off: (lambda bi, p, pt: (bi, 0, off // DIFF_W))
    page = lambda i: (lambda bi, p, pt: (pt[bi, p * g + i], 0, 0))
    page_spec = lambda i: pl.BlockSpec((1, prow, DIFF_DH), page(i))
    lam_spec = pl.BlockSpec((1, DIFF_DH), lambda bi, p, pt: (0, 0))
    grid_spec = pltpu.PrefetchScalarGridSpec(
        num_scalar_prefetch=1,
        grid=(b, n_pages // g),
        in_specs=[
            pl.BlockSpec((1, t, DIFF_W), zcol(OFF_DQ)),
            pl.BlockSpec((1, t, DIFF_W), zcol(OFF_DK)),
            pl.BlockSpec((1, t, DIFF_W), zcol(OFF_DV)),
            *[page_spec(i) for i in range(g)],
            *[page_spec(i) for i in range(g)],
            lam_spec, lam_spec, lam_spec, lam_spec],
        out_specs=pl.BlockSpec((1, t, DIFF_W), lambda bi, p, pt: (bi, 0, 0)),
        scratch_shapes=[pltpu.VMEM((nrow, DIFF_DH), F32),
                        pltpu.VMEM((nrow, LANES), F32),
                        pltpu.VMEM((nrow, LANES), F32),
                        pltpu.VMEM((nrow, DIFF_HW), F32),
                        pltpu.VMEM((2, prow, DIFF_DH), F32)])
    ck, cv = _page_rows(cache_k), _page_rows(cache_v)
    return pl.pallas_call(
        functools.partial(_diff_sample_kernel, t=t, g=g),
        grid_spec=grid_spec,
        out_shape=jax.ShapeDtypeStruct((b, t, DIFF_W), F32),
        compiler_params=_cparams("parallel", "arbitrary"),
        name="diff_sample",
    )(page_table, zs3, zs3, zs3, *([ck] * g), *([cv] * g), *lams)


def _mix_kernel(ro_ref, rg_ref, do_ref, ga_ref, gb_ref, gr_ref, gd_ref, wa_ref, wb_ref, m_ref, ry_ref, dy_ref):
    @pl.when(pl.program_id(1) == 0)
    def _():
        for h in range(RET_HEADS):
            sl = slice(h * RET_DV, (h + 1) * RET_DV)
            o = ro_ref[:, sl]
            y = o * lax.rsqrt(jnp.mean(o * o, axis=-1, keepdims=True) + EPS) * gr_ref[:, sl]
            g = rg_ref[:, sl]
            ry_ref[:, sl] = (y * (g * jax.nn.sigmoid(g))).astype(BF16)
        for h in range(DIFF_HEADS):
            sl = slice(h * DIFF_HW, (h + 1) * DIFF_HW)
            o = do_ref[:, sl]
            y = o * lax.rsqrt(jnp.mean(o * o, axis=-1, keepdims=True) + EPS) * gd_ref[...] * (1.0 - LAMBDA_INIT)
            dy_ref[:, sl] = y.astype(BF16)

    pa = _dot(ry_ref[...], wa_ref[...])
    pb = _dot(dy_ref[...], wb_ref[...])
    m = jax.nn.sigmoid(ga_ref[...]) * pa + jax.nn.sigmoid(gb_ref[...]) * pb
    m_ref[...] = m.astype(m_ref.dtype)


def _mixer_gate(z2, ret_o, diff_o, ret_norm_g, diff_norm_g, wa, wb, *, bm=512, bn=512):
    m = z2.shape[0]
    bm = min(bm, m)
    return pl.pallas_call(
        _mix_kernel,
        grid=(m // bm, D_MODEL // bn),
        in_specs=[
            pl.BlockSpec((bm, RET_W), lambda i, j: (i, 0)),
            pl.BlockSpec((bm, RET_W), lambda i, j: (i, OFF_RG // RET_W)),
            pl.BlockSpec((bm, DIFF_W), lambda i, j: (i, 0)),
            pl.BlockSpec((bm, bn), lambda i, j: (i, OFF_GA // bn + j)),
            pl.BlockSpec((bm, bn), lambda i, j: (i, OFF_GB // bn + j)),
            pl.BlockSpec((1, RET_W), lambda i, j: (0, 0)),
            pl.BlockSpec((1, DIFF_HW), lambda i, j: (0, 0)),
            pl.BlockSpec((RET_W, bn), lambda i, j: (0, j)),
            pl.BlockSpec((DIFF_W, bn), lambda i, j: (0, j))],
        out_specs=pl.BlockSpec((bm, bn), lambda i, j: (i, j)),
        out_shape=jax.ShapeDtypeStruct((m, D_MODEL), BF16),
        scratch_shapes=[pltpu.VMEM((bm, RET_W), BF16), pltpu.VMEM((bm, DIFF_W), BF16)],
        compiler_params=_cparams("parallel", "arbitrary"),
        name="mixer_gate",
    )(ret_o, z2, diff_o, z2, z2, ret_norm_g.reshape(1, RET_W), diff_norm_g.reshape(1, DIFF_HW), wa, wb)


def _xattn_kernel(q_ref, k_ref, v_ref, o_ref):
    scale = X_DH ** -0.5
    q = q_ref[0]
    k = k_ref[0]
    v = v_ref[0]
    for h in range(X_HEADS):
        sl = slice(h * X_DH, (h + 1) * X_DH)
        s = _dot_nt(q[:, sl].astype(BF16), k[:, sl].astype(BF16)) * scale
        s = s - jnp.max(s, axis=-1, keepdims=True)
        p = jnp.exp(s)
        p = p / jnp.sum(p, axis=-1, keepdims=True)
        o_ref[0, :, sl] = _dot(p.astype(BF16), v[:, sl].astype(BF16)).astype(o_ref.dtype)


def _cross_attention(q3, mem_k, mem_v, *, tq=512):
    b, s, w = q3.shape
    tq = min(tq, s)
    mk = mem_k.reshape(b, MEM_LEN, w)
    mv = mem_v.reshape(b, MEM_LEN, w)
    return pl.pallas_call(
        _xattn_kernel,
        grid=(b, s // tq),
        in_specs=[pl.BlockSpec((1, tq, w), lambda bi, i: (bi, i, 0)),
                  pl.BlockSpec((1, MEM_LEN, w), lambda bi, i: (bi, 0, 0)),
                  pl.BlockSpec((1, MEM_LEN, w), lambda bi, i: (bi, 0, 0))],
        out_specs=pl.BlockSpec((1, tq, w), lambda bi, i: (bi, i, 0)),
        out_shape=jax.ShapeDtypeStruct((b, s, w), BF16),
        compiler_params=_cparams("parallel", "parallel"),
        name="cross_attention",
    )(q3, mk, mv)


def _xattn_decode_kernel(q_ref, k_ref, v_ref, o_ref, qm_ref):
    scale = X_DH ** -0.5
    t = q_ref.shape[1]
    q = q_ref[0]
    for h in range(X_HEADS):
        qm_ref[h * t:(h + 1) * t, :] = q[:, h * X_DH:(h + 1) * X_DH]
    s = _dot_nt(qm_ref[...].astype(BF16), k_ref[0].astype(BF16)) * scale
    row = lax.broadcasted_iota(jnp.int32, s.shape, 0)
    col = lax.broadcasted_iota(jnp.int32, s.shape, 1)
    same_head = jnp.bitwise_and(col, X_HEADS - 1) == lax.shift_right_logical(row, t.bit_length() - 1)
    s = jnp.where(same_head, s, NEG_INF)
    s = s - jnp.max(s, axis=-1, keepdims=True)
    p = jnp.exp(s)
    p = p / jnp.sum(p, axis=-1, keepdims=True)
    o = _dot(p.astype(BF16), v_ref[0].astype(BF16))
    for h in range(X_HEADS):
        o_ref[0, :, h * X_DH:(h + 1) * X_DH] = o[h * t:(h + 1) * t].astype(o_ref.dtype)


def _cross_attention_decode(q3, mem_k, mem_v):
    b, t, w = q3.shape
    assert t & (t - 1) == 0 and X_HEADS & (X_HEADS - 1) == 0
    rows = MEM_LEN * X_HEADS
    mk = mem_k.reshape(b, rows, X_DH)
    mv = mem_v.reshape(b, rows, X_DH)
    return pl.pallas_call(
        _xattn_decode_kernel,
        grid=(b,),
        in_specs=[pl.BlockSpec((1, t, w), lambda bi: (bi, 0, 0)),
                  pl.BlockSpec((1, rows, X_DH), lambda bi: (bi, 0, 0)),
                  pl.BlockSpec((1, rows, X_DH), lambda bi: (bi, 0, 0))],
        out_specs=pl.BlockSpec((1, t, w), lambda bi: (bi, 0, 0)),
        out_shape=jax.ShapeDtypeStruct((b, t, w), BF16),
        scratch_shapes=[pltpu.VMEM((X_HEADS * t, X_DH), F32)],
        compiler_params=_cparams("parallel"),
        name="cross_attention_decode",
    )(q3, mk, mv)


def _topk_rows(s_ref, work_ref, rank_ref, sorted_ref, *, break_ties):
    shape = s_ref.shape
    nk = shape[0]
    iota = lax.broadcasted_iota(jnp.int32, shape, 0).astype(F32)
    work_ref[...] = s_ref[...]
    rank_ref[...] = jnp.full(shape, float(PEER_TOPK), F32)
    for r in range(PEER_TOPK):
        w = work_ref[...]
        m = jnp.max(w, axis=0, keepdims=True)
        sel = w == m
        if break_ties:
            sel = iota == jnp.min(jnp.where(sel, iota, float(nk)), axis=0, keepdims=True)
        rank_ref[...] = jnp.where(sel, float(r), rank_ref[...])
        work_ref[...] = jnp.where(sel, NEG_INF, w)
        sorted_ref[r:r + 1, :] = m


def _topk_exact(s_ref, work_ref, rank_ref, sorted_ref):
    _topk_rows(s_ref, work_ref, rank_ref, sorted_ref, break_ties=False)
    taken = jnp.sum(jnp.where(rank_ref[...] < PEER_TOPK, 1.0, 0.0), axis=0, keepdims=True)
    tied = jnp.max(jnp.where(taken == float(PEER_TOPK), 0.0, 1.0)) > 0.0

    @pl.when(tied)
    def _():
        _topk_rows(s_ref, work_ref, rank_ref, sorted_ref, break_ties=True)


def _route_kernel(q_ref, keys_ref, e1_ref, n1_ref, e2_ref, r2_ref,
                  st_ref, work_ref, rank1_ref, rank2_ref, s1s_ref, s2s_ref):
    q = q_ref[...]
    for c in range(2):
        qc = q[:, c * PEER_HALF:(c + 1) * PEER_HALF].astype(BF16)
        st_ref[c] = _dot_nt(keys_ref[c, 0].astype(BF16), qc)
    _topk_exact(st_ref.at[0], work_ref, rank1_ref, s1s_ref)
    _topk_exact(st_ref.at[1], work_ref, rank2_ref, s2s_ref)
    st = (st_ref[0], st_ref[1])
    s1 = s1s_ref[...]
    s2 = s2s_ref[...]

    iota = lax.broadcasted_iota(jnp.int32, s1.shape, 0).astype(F32)
    ptr = jnp.zeros(s1.shape, F32)
    count = jnp.zeros(s1.shape, F32)
    front = s1 + s2[0:1, :]
    zsum = jnp.zeros((1, s1.shape[1]), F32)
    top0 = None
    for kk in range(PEER_TOPK):
        m = jnp.max(front, axis=0, keepdims=True)
        if kk == 0:
            top0 = m
        zsum = zsum + jnp.exp(m - top0)
        istar = jnp.min(jnp.where(front == m, iota, float(PEER_TOPK)), axis=0, keepdims=True)
        oh = iota == istar
        count = count + jnp.where(oh, 1.0, 0.0)
        pnew = jnp.sum(jnp.where(oh, ptr, 0.0), axis=0, keepdims=True) + 1.0
        ptr = jnp.where(oh, pnew, ptr)
        s2n = jnp.max(jnp.where(iota == pnew, s2, NEG_INF), axis=0, keepdims=True)
        s1sel = jnp.max(jnp.where(oh, s1, NEG_INF), axis=0, keepdims=True)
        front = jnp.where(oh, s1sel + s2n, front)

    rank1 = rank1_ref[...]
    rank2 = rank2_ref[...]
    n1 = jnp.zeros(rank1.shape, F32)
    for r in range(PEER_TOPK):
        n1 = jnp.where(rank1 == float(r), count[r:r + 1, :], n1)
    inv_z = 1.0 / zsum
    e1_ref[0] = jnp.where(rank1 < PEER_TOPK, jnp.exp(st[0] - s1[0:1, :]) * inv_z, 0.0)
    e2_ref[0] = jnp.where(rank2 < PEER_TOPK, jnp.exp(st[1] - s2[0:1, :]), 0.0).astype(e2_ref.dtype)
    n1_ref[0] = n1
    r2_ref[0] = rank2.astype(r2_ref.dtype)


def _peer_route(qp, peer_keys, *, tt=512):
    t = qp.shape[0]
    tt = min(tt, t)
    hw = 2 * PEER_HALF
    wide = jax.ShapeDtypeStruct((PEER_HEADS, PEER_NKEYS, t), F32)
    narrow = jax.ShapeDtypeStruct((PEER_HEADS, PEER_NKEYS, t), BF16)
    ospec = pl.BlockSpec((1, PEER_NKEYS, tt), lambda i, h: (h, 0, i))
    return pl.pallas_call(
        _route_kernel,
        grid=(t // tt, PEER_HEADS),
        in_specs=[pl.BlockSpec((tt, hw), lambda i, h: (i, h)),
                  pl.BlockSpec((2, 1, PEER_NKEYS, PEER_HALF), lambda i, h: (0, h, 0, 0))],
        out_specs=[ospec, ospec, ospec, ospec],
        out_shape=[wide, wide, narrow, narrow],
        scratch_shapes=[pltpu.VMEM((2, PEER_NKEYS, tt), F32),
                        pltpu.VMEM((PEER_NKEYS, tt), F32), pltpu.VMEM((PEER_NKEYS, tt), F32),
                        pltpu.VMEM((PEER_NKEYS, tt), F32),
                        pltpu.VMEM((PEER_TOPK, tt), F32), pltpu.VMEM((PEER_TOPK, tt), F32)],
        compiler_params=_cparams("parallel", "parallel"),
        name="peer_route",
    )(qp, peer_keys)


def _transpose_tile_kernel(x_ref, o_ref):
    o_ref[0] = x_ref[...].T.astype(o_ref.dtype)


def _transposed_tiles(x, rows):
    n, d = x.shape
    return pl.pallas_call(
        _transpose_tile_kernel,
        grid=(n // rows,),
        in_specs=[pl.BlockSpec((rows, d), lambda i: (i, 0))],
        out_specs=pl.BlockSpec((1, d, rows), lambda i: (i, 0, 0)),
        out_shape=jax.ShapeDtypeStruct((n // rows, d, rows), BF16),
        compiler_params=_cparams("parallel"),
        name="transpose_tiles",
    )(x)


def _peer_dense_kernel(h2_ref, gf_ref, gl_ref, u_ref, vt_ref, e1_ref, n1_ref, e2_ref, r2_ref, y_ref,
                       xn_ref, acc_ref, ht0_ref, ht1_ref, *, et, n_tiles, th):
    s = pl.program_id(1)
    slabs = et // PEER_NKEYS
    tt = xn_ref.shape[0]
    ht_refs = (ht0_ref, ht1_ref)

    def project(par):
        ht_refs[par][...] = _dot_nt(u_ref[...], xn_ref[...])

    def gates(tile, c):
        cols = slice(c * th, (c + 1) * th)
        ws = []
        for sl in range(slabs):
            a = tile * slabs + sl
            w = None
            for h in range(PEER_HEADS):
                e1row = e1_ref[h, pl.ds(a, 1), cols].astype(BF16)
                n1row = n1_ref[h, pl.ds(a, 1), cols].astype(BF16)
                term = jnp.where(r2_ref[h, :, cols] < n1row, e2_ref[h, :, cols], 0.0) * e1row
                w = term if w is None else w + term
            ws.append(w)
        return ws[0] if slabs == 1 else jnp.concatenate(ws, axis=0)

    def finish(par):
        for c in range(tt // th):
            cols = slice(c * th, (c + 1) * th)
            act = jax.nn.gelu(ht_refs[par][:, cols]).astype(BF16)
            acc_ref[:, cols] += _dot(vt_ref[0], act * gates(s - 1, c))

    @pl.when(s == 0)
    def _():
        x = h2_ref[...]
        ms = jnp.mean(x * x, axis=-1, keepdims=True)
        xn_ref[...] = (x * lax.rsqrt(ms + EPS) * gf_ref[...]).astype(BF16)
        acc_ref[...] = jnp.zeros_like(acc_ref)
        project(0)

    for par in (0, 1):
        @pl.when(jnp.logical_and(jnp.logical_and(s > 0, s < n_tiles), lax.rem(s, 2) == par))
        def _():
            project(par)
            finish(1 - par)

    @pl.when(s == n_tiles)
    def _():
        finish((n_tiles - 1) % 2)
        r = h2_ref[...] + acc_ref[...].T
        ms = jnp.mean(r * r, axis=-1, keepdims=True)
        y_ref[...] = r * lax.rsqrt(ms + EPS) * gl_ref[...]


def _peer_dense(h2, g_ffn, g_final, u, vt, route, *, tt=512, th=256):
    t = h2.shape[0]
    tt = min(tt, t)
    n_tiles, _, et = vt.shape
    rspec = pl.BlockSpec((PEER_HEADS, PEER_NKEYS, tt), lambda i, s: (0, 0, i))
    return pl.pallas_call(
        functools.partial(_peer_dense_kernel, et=et, n_tiles=n_tiles, th=min(th, tt)),
        grid=(t // tt, n_tiles + 1),
        in_specs=[pl.BlockSpec((tt, D_MODEL), lambda i, s: (i, 0)),
                  pl.BlockSpec((1, D_MODEL), lambda i, s: (0, 0)),
                  pl.BlockSpec((1, D_MODEL), lambda i, s: (0, 0)),
                  pl.BlockSpec((et, D_MODEL), lambda i, s: (jnp.minimum(s, n_tiles - 1), 0)),
                  pl.BlockSpec((1, D_MODEL, et), lambda i, s: (jnp.maximum(s - 1, 0), 0, 0)),
                  rspec, rspec, rspec, rspec],
        out_specs=pl.BlockSpec((tt, D_MODEL), lambda i, s: (i, 0)),
        out_shape=jax.ShapeDtypeStruct((t, D_MODEL), F32),
        scratch_shapes=[pltpu.VMEM((tt, D_MODEL), BF16), pltpu.VMEM((D_MODEL, tt), F32),
                        pltpu.VMEM((et, tt), F32), pltpu.VMEM((et, tt), F32)],
        compiler_params=_cparams("parallel", "arbitrary", vmem=PEER_VMEM_LIMIT),
        name="peer_dense",
    )(h2, g_ffn.reshape(1, D_MODEL), g_final.reshape(1, D_MODEL), u, vt, *route)


def _tail(h1, mem_k, mem_v, w, batch):
    t = h1.shape[0]
    qx = _matmul(h1, w["xq"], gain=w["g_cross"], name="xq_proj")
    q3 = qx.reshape(batch, t // batch, X_HEADS * X_DH)
    attend = _cross_attention_decode if t // batch < SUBLANES else _cross_attention
    ox = attend(q3, mem_k, mem_v)
    h2 = _matmul(ox.reshape(t, X_HEADS * X_DH), w["xo"], residual=h1, name="xo_proj", bn=D_MODEL)
    qp = _matmul(h2, w["pq"], gain=w["g_ffn"], name="peer_query", bn=1024)
    route = _peer_route(qp, w["peer_keys"])
    return _peer_dense(h2, w["g_ffn"], w["g_final"], w["peer_u"], w["peer_vt"], route)


def _mixer_out(x2, z2, ret_o, diff_o, w):
    m = _mixer_gate(z2, ret_o, diff_o, w["ret_norm_g"], w["diff_norm_g"], w["branch_a"], w["branch_b"])
    return _matmul(m, w["out"], residual=x2, name="out_proj", bn=1024)


def kernel(x_prompt, x_sample, cache_k, cache_v, state_ret, cache_mem_k, cache_mem_v, page_table, mem_prompt, g_mix, w_in, ret_norm_g, diff_norm_g, lambda_q1, lambda_k1, lambda_q2, lambda_k2, w_branch_a, w_branch_b, w_out, g_cross, w_xq, w_mem_kv, w_xo, g_ffn, w_pq, peer_keys, peer_u, peer_v, g_final):
    bp, sp, d = x_prompt.shape
    bd, td, _ = x_sample.shape
    w = dict(
        ret_norm_g=ret_norm_g, diff_norm_g=diff_norm_g, g_cross=g_cross, g_ffn=g_ffn, g_final=g_final,
        branch_a=w_branch_a.astype(BF16), branch_b=w_branch_b.astype(BF16), out=w_out.astype(BF16),
        xq=w_xq.astype(BF16), xo=w_xo.astype(BF16), pq=w_pq.astype(BF16), peer_keys=peer_keys,
        peer_u=peer_u.astype(BF16), peer_vt=_transposed_tiles(peer_v, PEER_EXPERT_TILE))
    w_in_b = w_in.astype(BF16)
    lams = [a.reshape(1, DIFF_DH).astype(F32) for a in (lambda_q1, lambda_k1, lambda_q2, lambda_k2)]

    xp2 = x_prompt.reshape(bp * sp, d)
    z2, kp, vp = _matmul(xp2, w_in_b, gain=g_mix, name="in_proj", head_major=(OFF_DK, OFF_DV))
    z3 = z2.reshape(bp, sp, IN_W)
    ret_o, ret_state_prompt = _retention(z3, jnp.arange(sp, dtype=jnp.int32), None, RET_CHUNK, "retention_prompt")
    diff_o = _diff_prompt(z3, lams)
    h1 = _mixer_out(xp2, z2, ret_o.reshape(bp * sp, RET_W), diff_o.reshape(bp * sp, DIFF_W), w)
    mem_kv = _matmul(mem_prompt.reshape(bp * MEM_LEN, d), w_mem_kv.astype(BF16), name="mem_kv_proj")
    mem_kv = mem_kv.reshape(bp, MEM_LEN, 2, X_HEADS, X_DH)
    mem_k_prompt, mem_v_prompt = mem_kv[:, :, 0], mem_kv[:, :, 1]
    y_prompt = _tail(h1, mem_k_prompt, mem_v_prompt, w, bp).reshape(bp, sp, d)
    k_prompt, v_prompt = _heads_view(kp, bp), _heads_view(vp, bp)

    past = page_table.shape[1] * PAGE_SIZE
    xs2 = x_sample.reshape(bd * td, d)
    zs2, ks, vs = _matmul(xs2, w_in_b, gain=g_mix, name="in_proj", head_major=(OFF_DK, OFF_DV))
    zs3 = zs2.reshape(bd, td, IN_W)
    ret_o_s, ret_state_sample = _retention(zs3, past + jnp.arange(td, dtype=jnp.int32), state_ret, td,
                                           "retention_sample")
    diff_o_s = _diff_sample(zs3, cache_k, cache_v, page_table, lams)
    h1s = _mixer_out(xs2, zs2, ret_o_s.reshape(bd * td, RET_W), diff_o_s.reshape(bd * td, DIFF_W), w)
    y_sample = _tail(h1s, cache_mem_k, cache_mem_v, w, bd).reshape(bd, td, d)
    k_sample, v_sample = _heads_view(ks, bd), _heads_view(vs, bd)

    return (y_prompt, y_sample, k_prompt, v_prompt, ret_state_prompt, mem_k_prompt, mem_v_prompt,
            k_sample, v_sample, ret_state_sample)
```

```python
import functools
import math

import numpy as np
import jax
import jax.numpy as jnp
from jax import lax
from jax.experimental import pallas as pl
from jax.experimental.pallas import tpu as pltpu

F32 = jnp.float32
BF16 = jnp.bfloat16

D_MODEL = 2048
RET_HEADS = 8
RET_DK = 128
RET_DV = 128
RET_CHUNK = 128
DIFF_HEADS = 4
DIFF_DH = 128
DIFF_HW = 2 * DIFF_DH
PAGE_SIZE = 128
MEM_LEN = 256
X_HEADS = 4
X_DH = 128
PEER_HEADS = 8
PEER_NKEYS = 128
PEER_HALF = 128
PEER_TOPK = 16
ROPE_BASE = 10000.0
LAMBDA_INIT = 0.8 - 0.6 * math.exp(-0.3 * 0)
EPS = 1e-6

RET_W = RET_HEADS * RET_DK
DIFF_W = DIFF_HEADS * DIFF_HW
OFF_RQ, OFF_RK, OFF_RV, OFF_RG = 0, RET_W, 2 * RET_W, 3 * RET_W
OFF_DQ = 4 * RET_W
OFF_DK = OFF_DQ + DIFF_W
OFF_DV = OFF_DK + DIFF_W
OFF_GA = OFF_DV + DIFF_W
OFF_GB = OFF_GA + D_MODEL
IN_W = OFF_GB + D_MODEL

LANES = 128
SUBLANES = 8
VMEM_LIMIT = 48 * 1024 * 1024
PEER_VMEM_LIMIT = 58 * 1024 * 1024
PEER_EXPERT_TILE = 512
SAMPLE_PAGES_PER_STEP = 16

NEG_INF = float("-inf")


def _cparams(*sem, vmem=VMEM_LIMIT, flags=None):
    return pltpu.CompilerParams(dimension_semantics=sem, vmem_limit_bytes=vmem, flags=flags)


def _dot(a, b):
    return jnp.dot(a, b, preferred_element_type=F32)


def _dot_nt(a, b):
    return lax.dot_general(a, b, (((1,), (1,)), ((), ())), preferred_element_type=F32)


def _dot_tn(a, b):
    return lax.dot_general(a, b, (((0,), (0,)), ((), ())), preferred_element_type=F32)


def _mm_kernel(*refs, norm, residual, head_copies):
    it = iter(refs)
    x_ref = next(it)
    g_ref = next(it) if norm else None
    w_ref = next(it)
    r_ref = next(it) if residual else None
    o_ref = next(it)
    copy_refs = [next(it) for _ in head_copies]
    xs_ref = next(it)
    j = pl.program_id(1)

    @pl.when(j == 0)
    def _():
        x = x_ref[...].astype(F32)
        if norm:
            ms = jnp.mean(x * x, axis=-1, keepdims=True)
            x = x * lax.rsqrt(ms + EPS) * g_ref[...]
        xs_ref[...] = x.astype(BF16)

    acc = _dot(xs_ref[...], w_ref[...])
    if residual:
        acc = r_ref[...] + acc
    o_ref[...] = acc.astype(o_ref.dtype)
    chunks = acc.shape[1] // LANES
    for (first, count), c_ref in zip(head_copies, copy_refs):
        for q in range(count):
            @pl.when(j == first + q)
            def _():
                for p in range(chunks):
                    head, half = divmod(q * chunks + p, 2)
                    c_ref[:, half * DIFF_HEADS + head, :] = acc[:, p * LANES:(p + 1) * LANES]


def _matmul(x, w, *, name, gain=None, residual=None, out_dtype=F32, bm=1024, bn=512, head_major=()):
    m, k = x.shape
    n = w.shape[1]
    bm = min(bm, m)
    bn = min(bn, n)
    assert m % bm == 0 and n % bn == 0
    norm = gain is not None
    res = residual is not None
    head_copies = []
    out_specs = [pl.BlockSpec((bm, bn), lambda i, j: (i, j))]
    out_shape = [jax.ShapeDtypeStruct((m, n), out_dtype)]
    for off in head_major:
        assert off % bn == 0 and DIFF_W % bn == 0 and bn % DIFF_HW == 0
        head_copies.append((off // bn, DIFF_W // bn))
        out_specs.append(pl.BlockSpec((bm, 2 * DIFF_HEADS, DIFF_DH), lambda i, j: (i, 0, 0)))
        out_shape.append(jax.ShapeDtypeStruct((m, 2 * DIFF_HEADS, DIFF_DH), F32))
    in_specs = [pl.BlockSpec((bm, k), lambda i, j: (i, 0))]
    args = [x]
    if norm:
        in_specs.append(pl.BlockSpec((1, k), lambda i, j: (0, 0)))
        args.append(gain.reshape(1, k).astype(F32))
    in_specs.append(pl.BlockSpec((k, bn), lambda i, j: (0, j)))
    args.append(w)
    if res:
        in_specs.append(pl.BlockSpec((bm, bn), lambda i, j: (i, j)))
        args.append(residual)
    outs = pl.pallas_call(
        functools.partial(_mm_kernel, norm=norm, residual=res, head_copies=tuple(head_copies)),
        grid=(m // bm, n // bn),
        in_specs=in_specs,
        out_specs=out_specs,
        out_shape=out_shape,
        scratch_shapes=[pltpu.VMEM((bm, k), BF16)],
        compiler_params=_cparams("parallel", "arbitrary"),
        name=name,
    )(*args)
    return outs if head_major else outs[0]


def _heads_view(x, batch):
    s = x.shape[0] // batch
    x5 = x.reshape(batch, s, 2, DIFF_HEADS, DIFF_DH)
    return jnp.transpose(x5, (0, 1, 3, 2, 4)).reshape(batch, s, DIFF_HEADS, DIFF_HW)


def _ret_tables(chunk_len):
    c = RET_CHUNK
    lg = jnp.log1p(-jnp.exp2(-5.0 - jnp.arange(RET_HEADS, dtype=F32)))
    i = jnp.arange(c, dtype=F32)
    rel = i[:, None] - i[None, :]
    causal = rel >= 0
    dmat = jnp.where(causal[None], jnp.exp(jnp.where(causal, rel, 0.0)[None] * lg[:, None, None]), 0.0)
    dec_in = jnp.exp((i + 1.0)[None, :] * lg[:, None])
    dec_out = jnp.exp((chunk_len - 1.0 - i)[None, :] * lg[:, None])
    dec_chunk = jnp.exp(chunk_len * lg)
    ones = jnp.ones((RET_HEADS, c, LANES), F32)
    return (dmat, dec_in[:, :, None] * ones, dec_out[:, :, None] * ones,
            dec_chunk[:, None, None] * ones)


def _rope_tables(pos):
    half = RET_DK // 2
    freqs = jnp.exp(-math.log(ROPE_BASE) * jnp.arange(half, dtype=F32) / half)
    ang = pos.astype(F32)[:, None] * freqs[None, :]
    cos, sin = jnp.cos(ang), jnp.sin(ang)
    return jnp.concatenate([cos, cos], axis=-1), jnp.concatenate([-sin, sin], axis=-1)


def _ret_kernel(*refs, rows, has_state):
    it = iter(refs)
    q_ref, k_ref, v_ref, cos_ref, sin_ref = (next(it) for _ in range(5))
    dmat_ref, din_ref, dout_ref, dch_ref = (next(it) for _ in range(4))
    s0_ref = next(it) if has_state else None
    o_ref, sfin_ref, s_ref = next(it), next(it), next(it)
    pad_ref = next(it) if rows < RET_CHUNK else None
    n = pl.program_id(1)

    @pl.when(n == 0)
    def _():
        if has_state:
            s_ref[...] = s0_ref[0]
        else:
            s_ref[...] = jnp.zeros_like(s_ref)

    def full(x):
        if rows == RET_CHUNK:
            return x
        width = x.shape[1]
        pad_ref[:, 0:width] = jnp.zeros((RET_CHUNK, width), F32)
        pad_ref[0:rows, 0:width] = x
        return pad_ref[:, 0:width]

    def rot(x, cos, sin):
        return x * cos + pltpu.roll(x, RET_DK // 2, axis=1) * sin

    cos, sin = full(cos_ref[...]), full(sin_ref[...])
    q_all, k_all, v_all = full(q_ref[0]), full(k_ref[0]), full(v_ref[0])
    for h in range(RET_HEADS):
        sl = slice(h * RET_DK, (h + 1) * RET_DK)
        q = rot(q_all[:, sl], cos, sin)
        k = rot(k_all[:, sl], cos, sin) * (RET_DK ** -0.5)
        v = v_all[:, sl].astype(BF16)
        s_prev = s_ref[h]
        qb = q.astype(BF16)
        scores = _dot_nt(qb, k.astype(BF16)) * dmat_ref[h]
        out = _dot(scores.astype(BF16), v) + _dot(qb, s_prev.astype(BF16)) * din_ref[h]
        s_ref[h] = dch_ref[h] * s_prev + _dot_tn((k * dout_ref[h]).astype(BF16), v)
        o_ref[0, :, sl] = out if rows == RET_CHUNK else out[0:rows, :]

    @pl.when(n == pl.num_programs(1) - 1)
    def _():
        sfin_ref[0] = s_ref[...]


def _retention(z3, pos, state0, chunk_len, name):
    b, s, _ = z3.shape
    rows = min(s, RET_CHUNK)
    nchunks = s // rows
    cos, sin = _rope_tables(pos)
    dmat, din, dout, dch = _ret_tables(float(chunk_len))
    has_state = state0 is not None
    qkv_spec = lambda off: pl.BlockSpec((1, rows, RET_W), lambda bi, n: (bi, n, off // RET_W))
    tab_spec = pl.BlockSpec((RET_HEADS, RET_CHUNK, LANES), lambda bi, n: (0, 0, 0))
    state_spec = pl.BlockSpec((1, RET_HEADS, RET_DK, RET_DV), lambda bi, n: (bi, 0, 0, 0))
    in_specs = [qkv_spec(OFF_RQ), qkv_spec(OFF_RK), qkv_spec(OFF_RV),
                pl.BlockSpec((rows, LANES), lambda bi, n: (n, 0)),
                pl.BlockSpec((rows, LANES), lambda bi, n: (n, 0)),
                tab_spec, tab_spec, tab_spec, tab_spec]
    args = [z3, z3, z3, cos, sin, dmat, din, dout, dch]
    if has_state:
        in_specs.append(state_spec)
        args.append(state0)
    scratch = [pltpu.VMEM((RET_HEADS, RET_DK, RET_DV), F32)]
    if rows < RET_CHUNK:
        scratch.append(pltpu.VMEM((RET_CHUNK, RET_W), F32))
    return pl.pallas_call(
        functools.partial(_ret_kernel, rows=rows, has_state=has_state),
        grid=(b, nchunks),
        in_specs=in_specs,
        out_specs=[pl.BlockSpec((1, rows, RET_W), lambda bi, n: (bi, n, 0)), state_spec],
        out_shape=[jax.ShapeDtypeStruct((b, s, RET_W), F32),
                   jax.ShapeDtypeStruct((b, RET_HEADS, RET_DK, RET_DV), F32)],
        scratch_shapes=scratch,
        compiler_params=_cparams("parallel", "arbitrary"),
        name=name,
    )(*args)


def _lambda_value(lq1_ref, lk1_ref, lq2_ref, lk2_ref):
    a = jnp.sum(lq1_ref[...] * lk1_ref[...], axis=-1, keepdims=True)
    b = jnp.sum(lq2_ref[...] * lk2_ref[...], axis=-1, keepdims=True)
    return jnp.exp(a) - jnp.exp(b) + LAMBDA_INIT


def _online_update(s, pv, m_ref, l_ref, acc_ref, idx):
    def lanes(x, width):
        return x if width == LANES else jnp.concatenate([x] * (width // LANES), axis=1)

    m_prev = m_ref[idx]
    m_new = jnp.maximum(m_prev, jnp.max(s, axis=-1, keepdims=True))
    alpha = jnp.exp(m_prev - m_new)
    p = jnp.exp(s - lanes(m_new, s.shape[1]))
    l_ref[idx] = alpha * l_ref[idx] + jnp.sum(p, axis=-1, keepdims=True)
    acc = acc_ref[idx]
    acc_ref[idx] = lanes(alpha, acc.shape[1]) * acc + pv(p)
    m_ref[idx] = m_new


def _diff_prompt_kernel(qi_ref, ki_ref, q_ref, k_ref, v_ref, lq1, lk1, lq2, lk2, o_ref,
                        m_ref, l_ref, acc_ref):
    pair = pl.program_id(2)
    qi, ki = qi_ref[pair], ki_ref[pair]
    scale = DIFF_DH ** -0.5

    @pl.when(ki == 0)
    def _():
        m_ref[...] = jnp.full_like(m_ref, NEG_INF)
        l_ref[...] = jnp.zeros_like(l_ref)
        acc_ref[...] = jnp.zeros_like(acc_ref)

    def step(masked):
        q = q_ref[0]
        k = k_ref[0]
        v = v_ref[0].astype(BF16)
        for c in range(2):
            sl = slice(c * DIFF_DH, (c + 1) * DIFF_DH)
            s = _dot_nt(q[:, sl].astype(BF16), k[:, sl].astype(BF16)) * scale
            if masked:
                row = lax.broadcasted_iota(jnp.int32, s.shape, 0)
                col = lax.broadcasted_iota(jnp.int32, s.shape, 1)
                s = jnp.where(row >= col, s, NEG_INF)
            _online_update(s, lambda p: _dot(p.astype(BF16), v), m_ref, l_ref, acc_ref, c)

    @pl.when(ki < qi)
    def _():
        step(False)

    @pl.when(ki == qi)
    def _():
        step(True)
        lam = _lambda_value(lq1, lk1, lq2, lk2)
        o0 = acc_ref[0] / l_ref[0][:, 0:1]
        o1 = acc_ref[1] / l_ref[1][:, 0:1]
        o_ref[0] = o0 - lam * o1


def _diff_prompt(z3, lams, *, tq=512):
    b, s, _ = z3.shape
    nq = s // tq
    pairs = [(qi, ki) for qi in range(nq) for ki in range(qi + 1)]
    qi_tab = jnp.asarray([p[0] for p in pairs], jnp.int32)
    ki_tab = jnp.asarray([p[1] for p in pairs], jnp.int32)
    col = lambda off: off // DIFF_HW
    lam_spec = pl.BlockSpec((1, DIFF_DH), lambda bi, h, p, qt, kt: (0, 0))
    grid_spec = pltpu.PrefetchScalarGridSpec(
        num_scalar_prefetch=2,
        grid=(b, DIFF_HEADS, len(pairs)),
        in_specs=[
            pl.BlockSpec((1, tq, DIFF_HW), lambda bi, h, p, qt, kt: (bi, qt[p], col(OFF_DQ) + h)),
            pl.BlockSpec((1, tq, DIFF_HW), lambda bi, h, p, qt, kt: (bi, kt[p], col(OFF_DK) + h)),
            pl.BlockSpec((1, tq, DIFF_HW), lambda bi, h, p, qt, kt: (bi, kt[p], col(OFF_DV) + h)),
            lam_spec, lam_spec, lam_spec, lam_spec],
        out_specs=pl.BlockSpec((1, tq, DIFF_HW), lambda bi, h, p, qt, kt: (bi, qt[p], h)),
        scratch_shapes=[pltpu.VMEM((2, tq, LANES), F32), pltpu.VMEM((2, tq, LANES), F32),
                        pltpu.VMEM((2, tq, DIFF_HW), F32)])
    return pl.pallas_call(
        _diff_prompt_kernel,
        grid_spec=grid_spec,
        out_shape=jax.ShapeDtypeStruct((b, s, DIFF_W), F32),
        compiler_params=_cparams("parallel", "parallel", "arbitrary"),
        name="diff_prompt",
    )(qi_tab, ki_tab, z3, z3, z3, *lams)


def _diff_sample_kernel(pt_ref, q_ref, kn_ref, vn_ref, *rest, t, g):
    k_refs, v_refs = rest[:g], rest[g:2 * g]
    lq1, lk1, lq2, lk2, o_ref, qm_ref, m_ref, l_ref, acc_ref, pad_ref = rest[2 * g:]
    p = pl.program_id(1)
    scale = DIFF_DH ** -0.5
    hrows = 2 * t
    nrow = DIFF_HEADS * hrows
    slots = 2 * DIFF_HEADS
    prow = PAGE_SIZE * slots
    slot_bits, hrow_bits = slots.bit_length() - 1, hrows.bit_length() - 1

    @pl.when(p == 0)
    def _():
        m_ref[...] = jnp.full_like(m_ref, NEG_INF)
        l_ref[...] = jnp.zeros_like(l_ref)
        acc_ref[...] = jnp.zeros_like(acc_ref)
        q = q_ref[0]
        for h in range(DIFF_HEADS):
            for c in range(2):
                r0 = h * hrows + c * t
                lo = h * DIFF_HW + c * DIFF_DH
                qm_ref[r0:r0 + t, :] = q[:, lo:lo + DIFF_DH]

    qm = qm_ref[...].astype(BF16)
    row = lax.broadcasted_iota(jnp.int32, (nrow, prow), 0)
    col = lax.broadcasted_iota(jnp.int32, (nrow, prow), 1)
    row_half = jnp.where(jnp.bitwise_and(row, hrows - 1) >= t, DIFF_HEADS, 0)
    row_slot = row_half + lax.shift_right_logical(row, hrow_bits)
    match = jnp.bitwise_and(col, slots - 1) == row_slot

    def page_scores(kpage):
        s = _dot_nt(qm, kpage.astype(BF16)) * scale
        return jnp.where(match, s, NEG_INF)

    def page_values(vpage):
        v3 = vpage.reshape(PAGE_SIZE, slots, DIFF_DH)
        other = pltpu.roll(v3, DIFF_HEADS, 1).reshape(prow, DIFF_DH)
        return jnp.concatenate([vpage.astype(BF16), other.astype(BF16)], axis=1)

    def update(scores, values):
        def pv(pr):
            out = None
            for i, vm in enumerate(values):
                o = _dot(pr[:, i * prow:(i + 1) * prow].astype(BF16), vm)
                out = o if out is None else out + o
            return out
        s = scores[0] if len(scores) == 1 else jnp.concatenate(scores, axis=1)
        _online_update(s, pv, m_ref, l_ref, acc_ref, slice(None))

    update([page_scores(k[0]) for k in k_refs], [page_values(v[0]) for v in v_refs])

    @pl.when(p == pl.num_programs(1) - 1)
    def _():
        pad_ref[...] = jnp.zeros_like(pad_ref)
        kn, vn = kn_ref[0], vn_ref[0]
        for tok in range(t):
            for c in range(2):
                for h in range(DIFF_HEADS):
                    r = tok * slots + c * DIFF_HEADS + h
                    lo = h * DIFF_HW + c * DIFF_DH
                    pad_ref[0, r:r + 1, :] = kn[tok:tok + 1, lo:lo + DIFF_DH]
                    pad_ref[1, r:r + 1, :] = vn[tok:tok + 1, lo:lo + DIFF_DH]
        s = page_scores(pad_ref[0])
        causal = lax.shift_right_logical(col, slot_bits) <= jnp.bitwise_and(row, t - 1)
        update([jnp.where(causal, s, NEG_INF)], [page_values(pad_ref[1])])
        lam = _lambda_value(lq1, lk1, lq2, lk2)
        acc = acc_ref[...]
        swapped = jnp.concatenate([acc[:, DIFF_DH:], acc[:, :DIFF_DH]], axis=1)
        out_row = lax.broadcasted_iota(jnp.int32, acc.shape, 0)
        second_half = jnp.bitwise_and(out_row, hrows - 1) >= t
        o = jnp.where(second_half, swapped, acc) / l_ref[...][:, 0:1]
        for h in range(DIFF_HEADS):
            r0 = h * hrows
            o_ref[0, :, h * DIFF_HW:(h + 1) * DIFF_HW] = o[r0:r0 + t] - lam * o[r0 + t:r0 + hrows]


def _page_rows(cache):
    n_pool = cache.shape[0]
    c5 = cache.reshape(n_pool, PAGE_SIZE, DIFF_HEADS, 2, DIFF_DH)
    return jnp.transpose(c5, (0, 1, 3, 2, 4)).reshape(n_pool, PAGE_SIZE * 2 * DIFF_HEADS, DIFF_DH)


def _diff_sample(zs3, cache_k, cache_v, page_table, lams):
    b, t, _ = zs3.shape
    assert t & (t - 1) == 0
    n_pages = page_table.shape[1]
    g = math.gcd(SAMPLE_PAGES_PER_STEP, n_pages)
    nrow = DIFF_HEADS * 2 * t
    prow = PAGE_SIZE * 2 * DIFF_HEADS
    zcol = lambda off: (lambda bi, p, pt: (bi, 0, off // DIFF_W))
    page = lambda i: (lambda bi, p, pt: (pt[bi, p * g + i], 0, 0))
    page_spec = lambda i: pl.BlockSpec((1, prow, DIFF_DH), page(i))
    lam_spec = pl.BlockSpec((1, DIFF_DH), lambda bi, p, pt: (0, 0))
    grid_spec = pltpu.PrefetchScalarGridSpec(
        num_scalar_prefetch=1,
        grid=(b, n_pages // g),
        in_specs=[
            pl.BlockSpec((1, t, DIFF_W), zcol(OFF_DQ)),
            pl.BlockSpec((1, t, DIFF_W), zcol(OFF_DK)),
            pl.BlockSpec((1, t, DIFF_W), zcol(OFF_DV)),
            *[page_spec(i) for i in range(g)],
            *[page_spec(i) for i in range(g)],
            lam_spec, lam_spec, lam_spec, lam_spec],
        out_specs=pl.BlockSpec((1, t, DIFF_W), lambda bi, p, pt: (bi, 0, 0)),
        scratch_shapes=[pltpu.VMEM((nrow, DIFF_DH), F32),
                        pltpu.VMEM((nrow, LANES), F32),
                        pltpu.VMEM((nrow, LANES), F32),
                        pltpu.VMEM((nrow, DIFF_HW), F32),
                        pltpu.VMEM((2, prow, DIFF_DH), F32)])
    ck, cv = _page_rows(cache_k), _page_rows(cache_v)
    return pl.pallas_call(
        functools.partial(_diff_sample_kernel, t=t, g=g),
        grid_spec=grid_spec,
        out_shape=jax.ShapeDtypeStruct((b, t, DIFF_W), F32),
        compiler_params=_cparams("parallel", "arbitrary"),
        name="diff_sample",
    )(page_table, zs3, zs3, zs3, *([ck] * g), *([cv] * g), *lams)


def _mix_kernel(ro_ref, rg_ref, do_ref, ga_ref, gb_ref, gr_ref, gd_ref, wa_ref, wb_ref, m_ref, ry_ref, dy_ref):
    @pl.when(pl.program_id(1) == 0)
    def _():
        for h in range(RET_HEADS):
            sl = slice(h * RET_DV, (h + 1) * RET_DV)
            o = ro_ref[:, sl]
            y = o * lax.rsqrt(jnp.mean(o * o, axis=-1, keepdims=True) + EPS) * gr_ref[:, sl]
            g = rg_ref[:, sl]
            ry_ref[:, sl] = (y * (g * jax.nn.sigmoid(g))).astype(BF16)
        for h in range(DIFF_HEADS):
            sl = slice(h * DIFF_HW, (h + 1) * DIFF_HW)
            o = do_ref[:, sl]
            y = o * lax.rsqrt(jnp.mean(o * o, axis=-1, keepdims=True) + EPS) * gd_ref[...] * (1.0 - LAMBDA_INIT)
            dy_ref[:, sl] = y.astype(BF16)

    pa = _dot(ry_ref[...], wa_ref[...])
    pb = _dot(dy_ref[...], wb_ref[...])
    m = jax.nn.sigmoid(ga_ref[...]) * pa + jax.nn.sigmoid(gb_ref[...]) * pb
    m_ref[...] = m.astype(m_ref.dtype)


def _mixer_gate(z2, ret_o, diff_o, ret_norm_g, diff_norm_g, wa, wb, *, bm=512, bn=1024):
    m = z2.shape[0]
    bm = min(bm, m)
    return pl.pallas_call(
        _mix_kernel,
        grid=(m // bm, D_MODEL // bn),
        in_specs=[
            pl.BlockSpec((bm, RET_W), lambda i, j: (i, 0)),
            pl.BlockSpec((bm, RET_W), lambda i, j: (i, OFF_RG // RET_W)),
            pl.BlockSpec((bm, DIFF_W), lambda i, j: (i, 0)),
            pl.BlockSpec((bm, bn), lambda i, j: (i, OFF_GA // bn + j)),
            pl.BlockSpec((bm, bn), lambda i, j: (i, OFF_GB // bn + j)),
            pl.BlockSpec((1, RET_W), lambda i, j: (0, 0)),
            pl.BlockSpec((1, DIFF_HW), lambda i, j: (0, 0)),
            pl.BlockSpec((RET_W, bn), lambda i, j: (0, j)),
            pl.BlockSpec((DIFF_W, bn), lambda i, j: (0, j))],
        out_specs=pl.BlockSpec((bm, bn), lambda i, j: (i, j)),
        out_shape=jax.ShapeDtypeStruct((m, D_MODEL), BF16),
        scratch_shapes=[pltpu.VMEM((bm, RET_W), BF16), pltpu.VMEM((bm, DIFF_W), BF16)],
        compiler_params=_cparams("parallel", "arbitrary"),
        name="mixer_gate",
    )(ret_o, z2, diff_o, z2, z2, ret_norm_g.reshape(1, RET_W), diff_norm_g.reshape(1, DIFF_HW), wa, wb)


def _xattn_kernel(q_ref, k_ref, v_ref, o_ref):
    scale = X_DH ** -0.5
    q = q_ref[0]
    k = k_ref[0]
    v = v_ref[0]
    for h in range(X_HEADS):
        sl = slice(h * X_DH, (h + 1) * X_DH)
        s = _dot_nt(q[:, sl].astype(BF16), k[:, sl].astype(BF16)) * scale
        s = s - jnp.max(s, axis=-1, keepdims=True)
        p = jnp.exp(s)
        p = p / jnp.sum(p, axis=-1, keepdims=True)
        o_ref[0, :, sl] = _dot(p.astype(BF16), v[:, sl].astype(BF16)).astype(o_ref.dtype)


def _cross_attention(q3, mem_k, mem_v, *, tq=512):
    b, s, w = q3.shape
    tq = min(tq, s)
    mk = mem_k.reshape(b, MEM_LEN, w)
    mv = mem_v.reshape(b, MEM_LEN, w)
    return pl.pallas_call(
        _xattn_kernel,
        grid=(b, s // tq),
        in_specs=[pl.BlockSpec((1, tq, w), lambda bi, i: (bi, i, 0)),
                  pl.BlockSpec((1, MEM_LEN, w), lambda bi, i: (bi, 0, 0)),
                  pl.BlockSpec((1, MEM_LEN, w), lambda bi, i: (bi, 0, 0))],
        out_specs=pl.BlockSpec((1, tq, w), lambda bi, i: (bi, i, 0)),
        out_shape=jax.ShapeDtypeStruct((b, s, w), BF16),
        compiler_params=_cparams("parallel", "parallel"),
        name="cross_attention",
    )(q3, mk, mv)


def _xattn_decode_kernel(q_ref, k_ref, v_ref, o_ref, qm_ref):
    scale = X_DH ** -0.5
    t = q_ref.shape[1]
    q = q_ref[0]
    for h in range(X_HEADS):
        qm_ref[h * t:(h + 1) * t, :] = q[:, h * X_DH:(h + 1) * X_DH]
    s = _dot_nt(qm_ref[...].astype(BF16), k_ref[0].astype(BF16)) * scale
    row = lax.broadcasted_iota(jnp.int32, s.shape, 0)
    col = lax.broadcasted_iota(jnp.int32, s.shape, 1)
    same_head = jnp.bitwise_and(col, X_HEADS - 1) == lax.shift_right_logical(row, t.bit_length() - 1)
    s = jnp.where(same_head, s, NEG_INF)
    s = s - jnp.max(s, axis=-1, keepdims=True)
    p = jnp.exp(s)
    p = p / jnp.sum(p, axis=-1, keepdims=True)
    o = _dot(p.astype(BF16), v_ref[0].astype(BF16))
    for h in range(X_HEADS):
        o_ref[0, :, h * X_DH:(h + 1) * X_DH] = o[h * t:(h + 1) * t].astype(o_ref.dtype)


def _cross_attention_decode(q3, mem_k, mem_v):
    b, t, w = q3.shape
    assert t & (t - 1) == 0 and X_HEADS & (X_HEADS - 1) == 0
    rows = MEM_LEN * X_HEADS
    mk = mem_k.reshape(b, rows, X_DH)
    mv = mem_v.reshape(b, rows, X_DH)
    return pl.pallas_call(
        _xattn_decode_kernel,
        grid=(b,),
        in_specs=[pl.BlockSpec((1, t, w), lambda bi: (bi, 0, 0)),
                  pl.BlockSpec((1, rows, X_DH), lambda bi: (bi, 0, 0)),
                  pl.BlockSpec((1, rows, X_DH), lambda bi: (bi, 0, 0))],
        out_specs=pl.BlockSpec((1, t, w), lambda bi: (bi, 0, 0)),
        out_shape=jax.ShapeDtypeStruct((b, t, w), BF16),
        scratch_shapes=[pltpu.VMEM((X_HEADS * t, X_DH), F32)],
        compiler_params=_cparams("parallel"),
        name="cross_attention_decode",
    )(q3, mk, mv)


def _topk_rows(s_ref, work_ref, rank_ref, sorted_ref, *, break_ties):
    shape = s_ref.shape
    nk = shape[0]
    iota = lax.broadcasted_iota(jnp.int32, shape, 0).astype(F32)
    work_ref[...] = s_ref[...]
    rank_ref[...] = jnp.full(shape, float(PEER_TOPK), F32)
    for r in range(PEER_TOPK):
        w = work_ref[...]
        m = jnp.max(w, axis=0, keepdims=True)
        sel = w == m
        if break_ties:
            sel = iota == jnp.min(jnp.where(sel, iota, float(nk)), axis=0, keepdims=True)
        rank_ref[...] = jnp.where(sel, float(r), rank_ref[...])
        work_ref[...] = jnp.where(sel, NEG_INF, w)
        sorted_ref[r:r + 1, :] = m


def _topk_exact(s_ref, work_ref, rank_ref, sorted_ref):
    _topk_rows(s_ref, work_ref, rank_ref, sorted_ref, break_ties=False)
    taken = jnp.sum(jnp.where(rank_ref[...] < PEER_TOPK, 1.0, 0.0), axis=0, keepdims=True)
    tied = jnp.max(jnp.where(taken == float(PEER_TOPK), 0.0, 1.0)) > 0.0

    @pl.when(tied)
    def _():
        _topk_rows(s_ref, work_ref, rank_ref, sorted_ref, break_ties=True)


def _route_kernel(q_ref, keys_ref, e1_ref, n1_ref, e2_ref, r2_ref,
                  st_ref, work_ref, rank1_ref, rank2_ref, s1s_ref, s2s_ref):
    q = q_ref[...]
    for c in range(2):
        qc = q[:, c * PEER_HALF:(c + 1) * PEER_HALF].astype(BF16)
        st_ref[c] = _dot_nt(keys_ref[c, 0].astype(BF16), qc)
    _topk_exact(st_ref.at[0], work_ref, rank1_ref, s1s_ref)
    _topk_exact(st_ref.at[1], work_ref, rank2_ref, s2s_ref)
    st = (st_ref[0], st_ref[1])
    s1 = s1s_ref[...]
    s2 = s2s_ref[...]

    iota = lax.broadcasted_iota(jnp.int32, s1.shape, 0).astype(F32)
    ptr = jnp.zeros(s1.shape, F32)
    count = jnp.zeros(s1.shape, F32)
    front = s1 + s2[0:1, :]
    zsum = jnp.zeros((1, s1.shape[1]), F32)
    top0 = None
    for kk in range(PEER_TOPK):
        m = jnp.max(front, axis=0, keepdims=True)
        if kk == 0:
            top0 = m
        zsum = zsum + jnp.exp(m - top0)
        istar = jnp.min(jnp.where(front == m, iota, float(PEER_TOPK)), axis=0, keepdims=True)
        oh = iota == istar
        count = count + jnp.where(oh, 1.0, 0.0)
        pnew = jnp.sum(jnp.where(oh, ptr, 0.0), axis=0, keepdims=True) + 1.0
        ptr = jnp.where(oh, pnew, ptr)
        s2n = jnp.max(jnp.where(iota == pnew, s2, NEG_INF), axis=0, keepdims=True)
        s1sel = jnp.max(jnp.where(oh, s1, NEG_INF), axis=0, keepdims=True)
        front = jnp.where(oh, s1sel + s2n, front)

    rank1 = rank1_ref[...]
    rank2 = rank2_ref[...]
    n1 = jnp.zeros(rank1.shape, F32)
    for r in range(PEER_TOPK):
        n1 = jnp.where(rank1 == float(r), count[r:r + 1, :], n1)
    inv_z = 1.0 / zsum
    e1_ref[0] = jnp.where(rank1 < PEER_TOPK, jnp.exp(st[0] - s1[0:1, :]) * inv_z, 0.0)
    e2_ref[0] = jnp.where(rank2 < PEER_TOPK, jnp.exp(st[1] - s2[0:1, :]), 0.0).astype(e2_ref.dtype)
    n1_ref[0] = n1
    r2_ref[0] = rank2.astype(r2_ref.dtype)


def _peer_route(qp, peer_keys, *, tt=512):
    t = qp.shape[0]
    tt = min(tt, t)
    hw = 2 * PEER_HALF
    wide = jax.ShapeDtypeStruct((PEER_HEADS, PEER_NKEYS, t), F32)
    narrow = jax.ShapeDtypeStruct((PEER_HEADS, PEER_NKEYS, t), BF16)
    ospec = pl.BlockSpec((1, PEER_NKEYS, tt), lambda i, h: (h, 0, i))
    return pl.pallas_call(
        _route_kernel,
        grid=(t // tt, PEER_HEADS),
        in_specs=[pl.BlockSpec((tt, hw), lambda i, h: (i, h)),
                  pl.BlockSpec((2, 1, PEER_NKEYS, PEER_HALF), lambda i, h: (0, h, 0, 0))],
        out_specs=[ospec, ospec, ospec, ospec],
        out_shape=[wide, wide, narrow, narrow],
        scratch_shapes=[pltpu.VMEM((2, PEER_NKEYS, tt), F32),
                        pltpu.VMEM((PEER_NKEYS, tt), F32), pltpu.VMEM((PEER_NKEYS, tt), F32),
                        pltpu.VMEM((PEER_NKEYS, tt), F32),
                        pltpu.VMEM((PEER_TOPK, tt), F32), pltpu.VMEM((PEER_TOPK, tt), F32)],
        compiler_params=_cparams("parallel", "parallel"),
        name="peer_route",
    )(qp, peer_keys)


def _transpose_tile_kernel(x_ref, o_ref):
    o_ref[0] = x_ref[...].T.astype(o_ref.dtype)


def _transposed_tiles(x, rows):
    n, d = x.shape
    return pl.pallas_call(
        _transpose_tile_kernel,
        grid=(n // rows,),
        in_specs=[pl.BlockSpec((rows, d), lambda i: (i, 0))],
        out_specs=pl.BlockSpec((1, d, rows), lambda i: (i, 0, 0)),
        out_shape=jax.ShapeDtypeStruct((n // rows, d, rows), BF16),
        compiler_params=_cparams("parallel"),
        name="transpose_tiles",
    )(x)


def _peer_dense_kernel(h2_ref, gf_ref, gl_ref, u_ref, vt_ref, e1_ref, n1_ref, e2_ref, r2_ref, y_ref,
                       xn_ref, acc_ref, ht0_ref, ht1_ref, *, et, n_tiles, th):
    s = pl.program_id(1)
    slabs = et // PEER_NKEYS
    tt = xn_ref.shape[0]
    ht_refs = (ht0_ref, ht1_ref)

    def project(par):
        ht_refs[par][...] = _dot_nt(u_ref[...].astype(BF16), xn_ref[...])

    def gates(tile, c):
        cols = slice(c * th, (c + 1) * th)
        ws = []
        for sl in range(slabs):
            a = tile * slabs + sl
            w = None
            for h in range(PEER_HEADS):
                e1row = e1_ref[h, pl.ds(a, 1), cols].astype(BF16)
                n1row = n1_ref[h, pl.ds(a, 1), cols].astype(BF16)
                term = jnp.where(r2_ref[h, :, cols] < n1row, e2_ref[h, :, cols], 0.0) * e1row
                w = term if w is None else w + term
            ws.append(w)
        return ws[0] if slabs == 1 else jnp.concatenate(ws, axis=0)

    def finish(par):
        for c in range(tt // th):
            cols = slice(c * th, (c + 1) * th)
            act = jax.nn.gelu(ht_refs[par][:, cols]).astype(BF16)
            acc_ref[:, cols] += _dot(vt_ref[0], act * gates(s - 1, c))

    @pl.when(s == 0)
    def _():
        x = h2_ref[...]
        ms = jnp.mean(x * x, axis=-1, keepdims=True)
        xn_ref[...] = (x * lax.rsqrt(ms + EPS) * gf_ref[...]).astype(BF16)
        acc_ref[...] = jnp.zeros_like(acc_ref)
        project(0)

    for par in (0, 1):
        @pl.when(jnp.logical_and(jnp.logical_and(s > 0, s < n_tiles), lax.rem(s, 2) == par))
        def _():
            project(par)
            finish(1 - par)

    @pl.when(s == n_tiles)
    def _():
        finish((n_tiles - 1) % 2)
        r = h2_ref[...] + acc_ref[...].T
        ms = jnp.mean(r * r, axis=-1, keepdims=True)
        y_ref[...] = r * lax.rsqrt(ms + EPS) * gl_ref[...]


def _peer_dense(h2, g_ffn, g_final, u, vt, route, *, tt=512, th=256):
    t = h2.shape[0]
    tt = min(tt, t)
    n_tiles, _, et = vt.shape
    rspec = pl.BlockSpec((PEER_HEADS, PEER_NKEYS, tt), lambda i, s: (0, 0, i))
    return pl.pallas_call(
        functools.partial(_peer_dense_kernel, et=et, n_tiles=n_tiles, th=min(th, tt)),
        grid=(t // tt, n_tiles + 1),
        in_specs=[pl.BlockSpec((tt, D_MODEL), lambda i, s: (i, 0)),
                  pl.BlockSpec((1, D_MODEL), lambda i, s: (0, 0)),
                  pl.BlockSpec((1, D_MODEL), lambda i, s: (0, 0)),
                  pl.BlockSpec((et, D_MODEL), lambda i, s: (jnp.minimum(s, n_tiles - 1), 0)),
                  pl.BlockSpec((1, D_MODEL, et), lambda i, s: (jnp.maximum(s - 1, 0), 0, 0)),
                  rspec, rspec, rspec, rspec],
        out_specs=pl.BlockSpec((tt, D_MODEL), lambda i, s: (i, 0)),
        out_shape=jax.ShapeDtypeStruct((t, D_MODEL), F32),
        scratch_shapes=[pltpu.VMEM((tt, D_MODEL), BF16), pltpu.VMEM((D_MODEL, tt), F32),
                        pltpu.VMEM((et, tt), F32), pltpu.VMEM((et, tt), F32)],
        compiler_params=_cparams("parallel", "arbitrary", vmem=PEER_VMEM_LIMIT),
        name="peer_dense",
    )(h2, g_ffn.reshape(1, D_MODEL), g_final.reshape(1, D_MODEL), u, vt, *route)


def _tail(h1, mem_k, mem_v, w, batch):
    t = h1.shape[0]
    qx = _matmul(h1, w["xq"], gain=w["g_cross"], name="xq_proj")
    q3 = qx.reshape(batch, t // batch, X_HEADS * X_DH)
    attend = _cross_attention_decode if t // batch < SUBLANES else _cross_attention
    ox = attend(q3, mem_k, mem_v)
    h2 = _matmul(ox.reshape(t, X_HEADS * X_DH), w["xo"], residual=h1, name="xo_proj", bn=D_MODEL)
    qp = _matmul(h2, w["pq"], gain=w["g_ffn"], name="peer_query", bn=1024)
    route = _peer_route(qp, w["peer_keys"])
    return _peer_dense(h2, w["g_ffn"], w["g_final"], w["peer_u"], w["peer_vt"], route)


def _mixer_out(x2, z2, ret_o, diff_o, w):
    m = _mixer_gate(z2, ret_o, diff_o, w["ret_norm_g"], w["diff_norm_g"], w["branch_a"], w["branch_b"])
    return _matmul(m, w["out"], residual=x2, name="out_proj", bn=1024)


def kernel(x_prompt, x_sample, cache_k, cache_v, state_ret, cache_mem_k, cache_mem_v, page_table, mem_prompt, g_mix, w_in, ret_norm_g, diff_norm_g, lambda_q1, lambda_k1, lambda_q2, lambda_k2, w_branch_a, w_branch_b, w_out, g_cross, w_xq, w_mem_kv, w_xo, g_ffn, w_pq, peer_keys, peer_u, peer_v, g_final):
    bp, sp, d = x_prompt.shape
    bd, td, _ = x_sample.shape
    w = dict(
        ret_norm_g=ret_norm_g, diff_norm_g=diff_norm_g, g_cross=g_cross, g_ffn=g_ffn, g_final=g_final,
        branch_a=w_branch_a.astype(BF16), branch_b=w_branch_b.astype(BF16), out=w_out.astype(BF16),
        xq=w_xq.astype(BF16), xo=w_xo.astype(BF16), pq=w_pq.astype(BF16), peer_keys=peer_keys,
        peer_u=peer_u, peer_vt=_transposed_tiles(peer_v, PEER_EXPERT_TILE))
    w_in_b = w_in.astype(BF16)
    lams = [a.reshape(1, DIFF_DH).astype(F32) for a in (lambda_q1, lambda_k1, lambda_q2, lambda_k2)]

    xp2 = x_prompt.reshape(bp * sp, d)
    z2, kp, vp = _matmul(xp2, w_in_b, gain=g_mix, name="in_proj", head_major=(OFF_DK, OFF_DV))
    z3 = z2.reshape(bp, sp, IN_W)
    ret_o, ret_state_prompt = _retention(z3, jnp.arange(sp, dtype=jnp.int32), None, RET_CHUNK, "retention_prompt")
    diff_o = _diff_prompt(z3, lams)
    h1 = _mixer_out(xp2, z2, ret_o.reshape(bp * sp, RET_W), diff_o.reshape(bp * sp, DIFF_W), w)
    mem_kv = _matmul(mem_prompt.reshape(bp * MEM_LEN, d), w_mem_kv.astype(BF16), name="mem_kv_proj")
    mem_kv = mem_kv.reshape(bp, MEM_LEN, 2, X_HEADS, X_DH)
    mem_k_prompt, mem_v_prompt = mem_kv[:, :, 0], mem_kv[:, :, 1]
    y_prompt = _tail(h1, mem_k_prompt, mem_v_prompt, w, bp).reshape(bp, sp, d)
    k_prompt, v_prompt = _heads_view(kp, bp), _heads_view(vp, bp)

    past = page_table.shape[1] * PAGE_SIZE
    xs2 = x_sample.reshape(bd * td, d)
    zs2, ks, vs = _matmul(xs2, w_in_b, gain=g_mix, name="in_proj", head_major=(OFF_DK, OFF_DV))
    zs3 = zs2.reshape(bd, td, IN_W)
    ret_o_s, ret_state_sample = _retention(zs3, past + jnp.arange(td, dtype=jnp.int32), state_ret, td,
                                           "retention_sample")
    diff_o_s = _diff_sample(zs3, cache_k, cache_v, page_table, lams)
    h1s = _mixer_out(xs2, zs2, ret_o_s.reshape(bd * td, RET_W), diff_o_s.reshape(bd * td, DIFF_W), w)
    y_sample = _tail(h1s, cache_mem_k, cache_mem_v, w, bd).reshape(bd, td, d)
    k_sample, v_sample = _heads_view(ks, bd), _heads_view(vs, bd)

    return (y_prompt, y_sample, k_prompt, v_prompt, ret_state_prompt, mem_k_prompt, mem_v_prompt,
            k_sample, v_sample, ret_state_sample)
```

```python
import functools
import math

import numpy as np
import jax
import jax.numpy as jnp
from jax import lax
from jax.experimental import pallas as pl
from jax.experimental.pallas import tpu as pltpu

F32 = jnp.float32
BF16 = jnp.bfloat16

D_MODEL = 2048
RET_HEADS = 8
RET_DK = 128
RET_DV = 128
RET_CHUNK = 128
DIFF_HEADS = 4
DIFF_DH = 128
DIFF_HW = 2 * DIFF_DH
PAGE_SIZE = 128
MEM_LEN = 256
X_HEADS = 4
X_DH = 128
PEER_HEADS = 8
PEER_NKEYS = 128
PEER_HALF = 128
PEER_TOPK = 16
ROPE_BASE = 10000.0
LAMBDA_INIT = 0.8 - 0.6 * math.exp(-0.3 * 0)
EPS = 1e-6

RET_W = RET_HEADS * RET_DK
DIFF_W = DIFF_HEADS * DIFF_HW
OFF_RQ, OFF_RK, OFF_RV, OFF_RG = 0, RET_W, 2 * RET_W, 3 * RET_W
OFF_DQ = 4 * RET_W
OFF_DK = OFF_DQ + DIFF_W
OFF_DV = OFF_DK + DIFF_W
OFF_GA = OFF_DV + DIFF_W
OFF_GB = OFF_GA + D_MODEL
IN_W = OFF_GB + D_MODEL

LANES = 128
SUBLANES = 8
VMEM_LIMIT = 48 * 1024 * 1024
PEER_VMEM_LIMIT = 58 * 1024 * 1024
PEER_EXPERT_TILE = 512
SAMPLE_PAGES_PER_STEP = 16

NEG_INF = float("-inf")


def _cparams(*sem, vmem=VMEM_LIMIT, flags=None):
    return pltpu.CompilerParams(dimension_semantics=sem, vmem_limit_bytes=vmem, flags=flags)


def _dot(a, b):
    return jnp.dot(a, b, preferred_element_type=F32)


def _dot_nt(a, b):
    return lax.dot_general(a, b, (((1,), (1,)), ((), ())), preferred_element_type=F32)


def _dot_tn(a, b):
    return lax.dot_general(a, b, (((0,), (0,)), ((), ())), preferred_element_type=F32)


def _mm_kernel(*refs, norm, residual, head_copies):
    it = iter(refs)
    x_ref = next(it)
    g_ref = next(it) if norm else None
    w_ref = next(it)
    r_ref = next(it) if residual else None
    o_ref = next(it)
    copy_refs = [next(it) for _ in head_copies]
    xs_ref = next(it)
    j = pl.program_id(1)

    @pl.when(j == 0)
    def _():
        x = x_ref[...].astype(F32)
        if norm:
            ms = jnp.mean(x * x, axis=-1, keepdims=True)
            x = x * lax.rsqrt(ms + EPS) * g_ref[...]
        xs_ref[...] = x.astype(BF16)

    acc = _dot(xs_ref[...], w_ref[...])
    if residual:
        acc = r_ref[...] + acc
    o_ref[...] = acc.astype(o_ref.dtype)
    chunks = acc.shape[1] // LANES
    for (first, count), c_ref in zip(head_copies, copy_refs):
        for q in range(count):
            @pl.when(j == first + q)
            def _():
                for p in range(chunks):
                    head, half = divmod(q * chunks + p, 2)
                    c_ref[:, half * DIFF_HEADS + head, :] = acc[:, p * LANES:(p + 1) * LANES]


def _matmul(x, w, *, name, gain=None, residual=None, out_dtype=F32, bm=1024, bn=512, head_major=()):
    m, k = x.shape
    n = w.shape[1]
    bm = min(bm, m)
    bn = min(bn, n)
    assert m % bm == 0 and n % bn == 0
    norm = gain is not None
    res = residual is not None
    head_copies = []
    out_specs = [pl.BlockSpec((bm, bn), lambda i, j: (i, j))]
    out_shape = [jax.ShapeDtypeStruct((m, n), out_dtype)]
    for off in head_major:
        assert off % bn == 0 and DIFF_W % bn == 0 and bn % DIFF_HW == 0
        head_copies.append((off // bn, DIFF_W // bn))
        out_specs.append(pl.BlockSpec((bm, 2 * DIFF_HEADS, DIFF_DH), lambda i, j: (i, 0, 0)))
        out_shape.append(jax.ShapeDtypeStruct((m, 2 * DIFF_HEADS, DIFF_DH), F32))
    in_specs = [pl.BlockSpec((bm, k), lambda i, j: (i, 0))]
    args = [x]
    if norm:
        in_specs.append(pl.BlockSpec((1, k), lambda i, j: (0, 0)))
        args.append(gain.reshape(1, k).astype(F32))
    in_specs.append(pl.BlockSpec((k, bn), lambda i, j: (0, j)))
    args.append(w)
    if res:
        in_specs.append(pl.BlockSpec((bm, bn), lambda i, j: (i, j)))
        args.append(residual)
    outs = pl.pallas_call(
        functools.partial(_mm_kernel, norm=norm, residual=res, head_copies=tuple(head_copies)),
        grid=(m // bm, n // bn),
        in_specs=in_specs,
        out_specs=out_specs,
        out_shape=out_shape,
        scratch_shapes=[pltpu.VMEM((bm, k), BF16)],
        compiler_params=_cparams("parallel", "arbitrary"),
        name=name,
    )(*args)
    return outs if head_major else outs[0]


def _heads_view(x, batch):
    s = x.shape[0] // batch
    x5 = x.reshape(batch, s, 2, DIFF_HEADS, DIFF_DH)
    return jnp.transpose(x5, (0, 1, 3, 2, 4)).reshape(batch, s, DIFF_HEADS, DIFF_HW)


def _ret_tables(chunk_len):
    c = RET_CHUNK
    lg = jnp.log1p(-jnp.exp2(-5.0 - jnp.arange(RET_HEADS, dtype=F32)))
    i = jnp.arange(c, dtype=F32)
    rel = i[:, None] - i[None, :]
    causal = rel >= 0
    dmat = jnp.where(causal[None], jnp.exp(jnp.where(causal, rel, 0.0)[None] * lg[:, None, None]), 0.0)
    dec_in = jnp.exp((i + 1.0)[None, :] * lg[:, None])
    dec_out = jnp.exp((chunk_len - 1.0 - i)[None, :] * lg[:, None])
    dec_chunk = jnp.exp(chunk_len * lg)
    ones = jnp.ones((RET_HEADS, c, LANES), F32)
    return (dmat, dec_in[:, :, None] * ones, dec_out[:, :, None] * ones,
            dec_chunk[:, None, None] * ones)


def _rope_tables(pos):
    half = RET_DK // 2
    freqs = jnp.exp(-math.log(ROPE_BASE) * jnp.arange(half, dtype=F32) / half)
    ang = pos.astype(F32)[:, None] * freqs[None, :]
    cos, sin = jnp.cos(ang), jnp.sin(ang)
    return jnp.concatenate([cos, cos], axis=-1), jnp.concatenate([-sin, sin], axis=-1)


def _ret_kernel(*refs, rows, has_state):
    it = iter(refs)
    q_ref, k_ref, v_ref, cos_ref, sin_ref = (next(it) for _ in range(5))
    dmat_ref, din_ref, dout_ref, dch_ref = (next(it) for _ in range(4))
    s0_ref = next(it) if has_state else None
    o_ref, sfin_ref, s_ref = next(it), next(it), next(it)
    pad_ref = next(it) if rows < RET_CHUNK else None
    n = pl.program_id(1)

    @pl.when(n == 0)
    def _():
        if has_state:
            s_ref[...] = s0_ref[0]
        else:
            s_ref[...] = jnp.zeros_like(s_ref)

    def full(x):
        if rows == RET_CHUNK:
            return x
        width = x.shape[1]
        pad_ref[:, 0:width] = jnp.zeros((RET_CHUNK, width), F32)
        pad_ref[0:rows, 0:width] = x
        return pad_ref[:, 0:width]

    def rot(x, cos, sin):
        return x * cos + pltpu.roll(x, RET_DK // 2, axis=1) * sin

    cos, sin = full(cos_ref[...]), full(sin_ref[...])
    q_all, k_all, v_all = full(q_ref[0]), full(k_ref[0]), full(v_ref[0])
    for h in range(RET_HEADS):
        sl = slice(h * RET_DK, (h + 1) * RET_DK)
        q = rot(q_all[:, sl], cos, sin)
        k = rot(k_all[:, sl], cos, sin) * (RET_DK ** -0.5)
        v = v_all[:, sl].astype(BF16)
        s_prev = s_ref[h]
        qb = q.astype(BF16)
        scores = _dot_nt(qb, k.astype(BF16)) * dmat_ref[h]
        out = _dot(scores.astype(BF16), v) + _dot(qb, s_prev.astype(BF16)) * din_ref[h]
        s_ref[h] = dch_ref[h] * s_prev + _dot_tn((k * dout_ref[h]).astype(BF16), v)
        o_ref[0, :, sl] = out if rows == RET_CHUNK else out[0:rows, :]

    @pl.when(n == pl.num_programs(1) - 1)
    def _():
        sfin_ref[0] = s_ref[...]


def _retention(z3, pos, state0, chunk_len, name):
    b, s, _ = z3.shape
    rows = min(s, RET_CHUNK)
    nchunks = s // rows
    cos, sin = _rope_tables(pos)
    dmat, din, dout, dch = _ret_tables(float(chunk_len))
    has_state = state0 is not None
    qkv_spec = lambda off: pl.BlockSpec((1, rows, RET_W), lambda bi, n: (bi, n, off // RET_W))
    tab_spec = pl.BlockSpec((RET_HEADS, RET_CHUNK, LANES), lambda bi, n: (0, 0, 0))
    state_spec = pl.BlockSpec((1, RET_HEADS, RET_DK, RET_DV), lambda bi, n: (bi, 0, 0, 0))
    in_specs = [qkv_spec(OFF_RQ), qkv_spec(OFF_RK), qkv_spec(OFF_RV),
                pl.BlockSpec((rows, LANES), lambda bi, n: (n, 0)),
                pl.BlockSpec((rows, LANES), lambda bi, n: (n, 0)),
                tab_spec, tab_spec, tab_spec, tab_spec]
    args = [z3, z3, z3, cos, sin, dmat, din, dout, dch]
    if has_state:
        in_specs.append(state_spec)
        args.append(state0)
    scratch = [pltpu.VMEM((RET_HEADS, RET_DK, RET_DV), F32)]
    if rows < RET_CHUNK:
        scratch.append(pltpu.VMEM((RET_CHUNK, RET_W), F32))
    return pl.pallas_call(
        functools.partial(_ret_kernel, rows=rows, has_state=has_state),
        grid=(b, nchunks),
        in_specs=in_specs,
        out_specs=[pl.BlockSpec((1, rows, RET_W), lambda bi, n: (bi, n, 0)), state_spec],
        out_shape=[jax.ShapeDtypeStruct((b, s, RET_W), F32),
                   jax.ShapeDtypeStruct((b, RET_HEADS, RET_DK, RET_DV), F32)],
        scratch_shapes=scratch,
        compiler_params=_cparams("parallel", "arbitrary"),
        name=name,
    )(*args)


def _lambda_value(lq1_ref, lk1_ref, lq2_ref, lk2_ref):
    a = jnp.sum(lq1_ref[...] * lk1_ref[...], axis=-1, keepdims=True)
    b = jnp.sum(lq2_ref[...] * lk2_ref[...], axis=-1, keepdims=True)
    return jnp.exp(a) - jnp.exp(b) + LAMBDA_INIT


def _online_update(s, pv, m_ref, l_ref, acc_ref, idx):
    def lanes(x, width):
        return x if width == LANES else jnp.concatenate([x] * (width // LANES), axis=1)

    m_prev = m_ref[idx]
    m_new = jnp.maximum(m_prev, jnp.max(s, axis=-1, keepdims=True))
    alpha = jnp.exp(m_prev - m_new)
    p = jnp.exp(s - lanes(m_new, s.shape[1]))
    l_ref[idx] = alpha * l_ref[idx] + jnp.sum(p, axis=-1, keepdims=True)
    acc = acc_ref[idx]
    acc_ref[idx] = lanes(alpha, acc.shape[1]) * acc + pv(p)
    m_ref[idx] = m_new


def _diff_prompt_kernel(qi_ref, ki_ref, q_ref, k_ref, v_ref, lq1, lk1, lq2, lk2, o_ref,
                        m_ref, l_ref, acc_ref):
    pair = pl.program_id(2)
    qi, ki = qi_ref[pair], ki_ref[pair]
    scale = DIFF_DH ** -0.5

    @pl.when(ki == 0)
    def _():
        m_ref[...] = jnp.full_like(m_ref, NEG_INF)
        l_ref[...] = jnp.zeros_like(l_ref)
        acc_ref[...] = jnp.zeros_like(acc_ref)

    def step(masked):
        q = q_ref[0]
        k = k_ref[0]
        v = v_ref[0].astype(BF16)
        for c in range(2):
            sl = slice(c * DIFF_DH, (c + 1) * DIFF_DH)
            s = _dot_nt(q[:, sl].astype(BF16), k[:, sl].astype(BF16)) * scale
            if masked:
                row = lax.broadcasted_iota(jnp.int32, s.shape, 0)
                col = lax.broadcasted_iota(jnp.int32, s.shape, 1)
                s = jnp.where(row >= col, s, NEG_INF)
            _online_update(s, lambda p: _dot(p.astype(BF16), v), m_ref, l_ref, acc_ref, c)

    @pl.when(ki < qi)
    def _():
        step(False)

    @pl.when(ki == qi)
    def _():
        step(True)
        lam = _lambda_value(lq1, lk1, lq2, lk2)
        o0 = acc_ref[0] / l_ref[0][:, 0:1]
        o1 = acc_ref[1] / l_ref[1][:, 0:1]
        o_ref[0] = o0 - lam * o1


def _diff_prompt(z3, lams, *, tq=512):
    b, s, _ = z3.shape
    nq = s // tq
    pairs = [(qi, ki) for qi in range(nq) for ki in range(qi + 1)]
    qi_tab = jnp.asarray([p[0] for p in pairs], jnp.int32)
    ki_tab = jnp.asarray([p[1] for p in pairs], jnp.int32)
    col = lambda off: off // DIFF_HW
    lam_spec = pl.BlockSpec((1, DIFF_DH), lambda bi, h, p, qt, kt: (0, 0))
    grid_spec = pltpu.PrefetchScalarGridSpec(
        num_scalar_prefetch=2,
        grid=(b, DIFF_HEADS, len(pairs)),
        in_specs=[
            pl.BlockSpec((1, tq, DIFF_HW), lambda bi, h, p, qt, kt: (bi, qt[p], col(OFF_DQ) + h)),
            pl.BlockSpec((1, tq, DIFF_HW), lambda bi, h, p, qt, kt: (bi, kt[p], col(OFF_DK) + h)),
            pl.BlockSpec((1, tq, DIFF_HW), lambda bi, h, p, qt, kt: (bi, kt[p], col(OFF_DV) + h)),
            lam_spec, lam_spec, lam_spec, lam_spec],
        out_specs=pl.BlockSpec((1, tq, DIFF_HW), lambda bi, h, p, qt, kt: (bi, qt[p], h)),
        scratch_shapes=[pltpu.VMEM((2, tq, LANES), F32), pltpu.VMEM((2, tq, LANES), F32),
                        pltpu.VMEM((2, tq, DIFF_HW), F32)])
    return pl.pallas_call(
        _diff_prompt_kernel,
        grid_spec=grid_spec,
        out_shape=jax.ShapeDtypeStruct((b, s, DIFF_W), F32),
        compiler_params=_cparams("parallel", "parallel", "arbitrary"),
        name="diff_prompt",
    )(qi_tab, ki_tab, z3, z3, z3, *lams)


def _diff_sample_kernel(pt_ref, q_ref, kn_ref, vn_ref, *rest, t, g):
    k_refs, v_refs = rest[:g], rest[g:2 * g]
    lq1, lk1, lq2, lk2, o_ref, qm_ref, m_ref, l_ref, acc_ref, pad_ref = rest[2 * g:]
    p = pl.program_id(1)
    scale = DIFF_DH ** -0.5
    hrows = 2 * t
    nrow = DIFF_HEADS * hrows
    slots = 2 * DIFF_HEADS
    prow = PAGE_SIZE * slots
    slot_bits, hrow_bits = slots.bit_length() - 1, hrows.bit_length() - 1

    @pl.when(p == 0)
    def _():
        m_ref[...] = jnp.full_like(m_ref, NEG_INF)
        l_ref[...] = jnp.zeros_like(l_ref)
        acc_ref[...] = jnp.zeros_like(acc_ref)
        q = q_ref[0]
        for h in range(DIFF_HEADS):
            for c in range(2):
                r0 = h * hrows + c * t
                lo = h * DIFF_HW + c * DIFF_DH
                qm_ref[r0:r0 + t, :] = q[:, lo:lo + DIFF_DH]

    qm = qm_ref[...].astype(BF16)
    row = lax.broadcasted_iota(jnp.int32, (nrow, prow), 0)
    col = lax.broadcasted_iota(jnp.int32, (nrow, prow), 1)
    row_half = jnp.where(jnp.bitwise_and(row, hrows - 1) >= t, DIFF_HEADS, 0)
    row_slot = row_half + lax.shift_right_logical(row, hrow_bits)
    match = jnp.bitwise_and(col, slots - 1) == row_slot

    def page_scores(kpage):
        s = _dot_nt(qm, kpage.astype(BF16)) * scale
        return jnp.where(match, s, NEG_INF)

    def page_values(vpage):
        v3 = vpage.reshape(PAGE_SIZE, slots, DIFF_DH)
        other = pltpu.roll(v3, DIFF_HEADS, 1).reshape(prow, DIFF_DH)
        return jnp.concatenate([vpage.astype(BF16), other.astype(BF16)], axis=1)

    def update(scores, values):
        def pv(pr):
            out = None
            for i, vm in enumerate(values):
                o = _dot(pr[:, i * prow:(i + 1) * prow].astype(BF16), vm)
                out = o if out is None else out + o
            return out
        s = scores[0] if len(scores) == 1 else jnp.concatenate(scores, axis=1)
        _online_update(s, pv, m_ref, l_ref, acc_ref, slice(None))

    update([page_scores(k[0]) for k in k_refs], [page_values(v[0]) for v in v_refs])

    @pl.when(p == pl.num_programs(1) - 1)
    def _():
        pad_ref[...] = jnp.zeros_like(pad_ref)
        kn, vn = kn_ref[0], vn_ref[0]
        for tok in range(t):
            for c in range(2):
                for h in range(DIFF_HEADS):
                    r = tok * slots + c * DIFF_HEADS + h
                    lo = h * DIFF_HW + c * DIFF_DH
                    pad_ref[0, r:r + 1, :] = kn[tok:tok + 1, lo:lo + DIFF_DH]
                    pad_ref[1, r:r + 1, :] = vn[tok:tok + 1, lo:lo + DIFF_DH]
        s = page_scores(pad_ref[0])
        causal = lax.shift_right_logical(col, slot_bits) <= jnp.bitwise_and(row, t - 1)
        update([jnp.where(causal, s, NEG_INF)], [page_values(pad_ref[1])])
        lam = _lambda_value(lq1, lk1, lq2, lk2)
        acc = acc_ref[...]
        swapped = jnp.concatenate([acc[:, DIFF_DH:], acc[:, :DIFF_DH]], axis=1)
        out_row = lax.broadcasted_iota(jnp.int32, acc.shape, 0)
        second_half = jnp.bitwise_and(out_row, hrows - 1) >= t
        o = jnp.where(second_half, swapped, acc) / l_ref[...][:, 0:1]
        for h in range(DIFF_HEADS):
            r0 = h * hrows
            o_ref[0, :, h * DIFF_HW:(h + 1) * DIFF_HW] = o[r0:r0 + t] - lam * o[r0 + t:r0 + hrows]


def _page_rows(cache):
    n_pool = cache.shape[0]
    c5 = cache.reshape(n_pool, PAGE_SIZE, DIFF_HEADS, 2, DIFF_DH)
    return jnp.transpose(c5, (0, 1, 3, 2, 4)).reshape(n_pool, PAGE_SIZE * 2 * DIFF_HEADS, DIFF_DH)


def _diff_sample(zs3, cache_k, cache_v, page_table, lams):
    b, t, _ = zs3.shape
    assert t & (t - 1) == 0
    n_pages = page_table.shape[1]
    g = math.gcd(SAMPLE_PAGES_PER_STEP, n_pages)
    nrow = DIFF_HEADS * 2 * t
    prow = PAGE_SIZE * 2 * DIFF_HEADS
    zcol = lambda off: (lambda bi, p, pt: (bi, 0, off // DIFF_W))
    page = lambda i: (lambda bi, p, pt: (pt[bi, p * g + i], 0, 0))
    page_spec = lambda i: pl.BlockSpec((1, prow, DIFF_DH), page(i))
    lam_spec = pl.BlockSpec((1, DIFF_DH), lambda bi, p, pt: (0, 0))
    grid_spec = pltpu.PrefetchScalarGridSpec(
        num_scalar_prefetch=1,
        grid=(b, n_pages // g),
        in_specs=[
            pl.BlockSpec((1, t, DIFF_W), zcol(OFF_DQ)),
            pl.BlockSpec((1, t, DIFF_W), zcol(OFF_DK)),
            pl.BlockSpec((1, t, DIFF_W), zcol(OFF_DV)),
            *[page_spec(i) for i in range(g)],
            *[page_spec(i) for i in range(g)],
            lam_spec, lam_spec, lam_spec, lam_spec],
        out_specs=pl.BlockSpec((1, t, DIFF_W), lambda bi, p, pt: (bi, 0, 0)),
        scratch_shapes=[pltpu.VMEM((nrow, DIFF_DH), F32),
                        pltpu.VMEM((nrow, LANES), F32),
                        pltpu.VMEM((nrow, LANES), F32),
                        pltpu.VMEM((nrow, DIFF_HW), F32),
                        pltpu.VMEM((2, prow, DIFF_DH), F32)])
    ck, cv = _page_rows(cache_k), _page_rows(cache_v)
    return pl.pallas_call(
        functools.partial(_diff_sample_kernel, t=t, g=g),
        grid_spec=grid_spec,
        out_shape=jax.ShapeDtypeStruct((b, t, DIFF_W), F32),
        compiler_params=_cparams("parallel", "arbitrary"),
        name="diff_sample",
    )(page_table, zs3, zs3, zs3, *([ck] * g), *([cv] * g), *lams)


def _mix_kernel(ro_ref, rg_ref, do_ref, ga_ref, gb_ref, gr_ref, gd_ref, wa_ref, wb_ref, m_ref, ry_ref, dy_ref):
    @pl.when(pl.program_id(1) == 0)
    def _():
        for h in range(RET_HEADS):
            sl = slice(h * RET_DV, (h + 1) * RET_DV)
            o = ro_ref[:, sl]
            y = o * lax.rsqrt(jnp.mean(o * o, axis=-1, keepdims=True) + EPS) * gr_ref[:, sl]
            g = rg_ref[:, sl]
            ry_ref[:, sl] = (y * (g * jax.nn.sigmoid(g))).astype(BF16)
        for h in range(DIFF_HEADS):
            sl = slice(h * DIFF_HW, (h + 1) * DIFF_HW)
            o = do_ref[:, sl]
            y = o * lax.rsqrt(jnp.mean(o * o, axis=-1, keepdims=True) + EPS) * gd_ref[...] * (1.0 - LAMBDA_INIT)
            dy_ref[:, sl] = y.astype(BF16)

    pa = _dot(ry_ref[...], wa_ref[...])
    pb = _dot(dy_ref[...], wb_ref[...])
    m = jax.nn.sigmoid(ga_ref[...]) * pa + jax.nn.sigmoid(gb_ref[...]) * pb
    m_ref[...] = m.astype(m_ref.dtype)


def _mixer_gate(z2, ret_o, diff_o, ret_norm_g, diff_norm_g, wa, wb, *, bm=512, bn=1024):
    m = z2.shape[0]
    bm = min(bm, m)
    return pl.pallas_call(
        _mix_kernel,
        grid=(m // bm, D_MODEL // bn),
        in_specs=[
            pl.BlockSpec((bm, RET_W), lambda i, j: (i, 0)),
            pl.BlockSpec((bm, RET_W), lambda i, j: (i, OFF_RG // RET_W)),
            pl.BlockSpec((bm, DIFF_W), lambda i, j: (i, 0)),
            pl.BlockSpec((bm, bn), lambda i, j: (i, OFF_GA // bn + j)),
            pl.BlockSpec((bm, bn), lambda i, j: (i, OFF_GB // bn + j)),
            pl.BlockSpec((1, RET_W), lambda i, j: (0, 0)),
            pl.BlockSpec((1, DIFF_HW), lambda i, j: (0, 0)),
            pl.BlockSpec((RET_W, bn), lambda i, j: (0, j)),
            pl.BlockSpec((DIFF_W, bn), lambda i, j: (0, j))],
        out_specs=pl.BlockSpec((bm, bn), lambda i, j: (i, j)),
        out_shape=jax.ShapeDtypeStruct((m, D_MODEL), BF16),
        scratch_shapes=[pltpu.VMEM((bm, RET_W), BF16), pltpu.VMEM((bm, DIFF_W), BF16)],
        compiler_params=_cparams("parallel", "arbitrary"),
        name="mixer_gate",
    )(ret_o, z2, diff_o, z2, z2, ret_norm_g.reshape(1, RET_W), diff_norm_g.reshape(1, DIFF_HW), wa, wb)


def _xattn_kernel(q_ref, k_ref, v_ref, o_ref):
    scale = X_DH ** -0.5
    q = q_ref[0]
    k = k_ref[0]
    v = v_ref[0]
    for h in range(X_HEADS):
        sl = slice(h * X_DH, (h + 1) * X_DH)
        s = _dot_nt(q[:, sl].astype(BF16), k[:, sl].astype(BF16)) * scale
        s = s - jnp.max(s, axis=-1, keepdims=True)
        p = jnp.exp(s)
        p = p / jnp.sum(p, axis=-1, keepdims=True)
        o_ref[0, :, sl] = _dot(p.astype(BF16), v[:, sl].astype(BF16)).astype(o_ref.dtype)


def _cross_attention(q3, mem_k, mem_v, *, tq=512):
    b, s, w = q3.shape
    tq = min(tq, s)
    mk = mem_k.reshape(b, MEM_LEN, w)
    mv = mem_v.reshape(b, MEM_LEN, w)
    return pl.pallas_call(
        _xattn_kernel,
        grid=(b, s // tq),
        in_specs=[pl.BlockSpec((1, tq, w), lambda bi, i: (bi, i, 0)),
                  pl.BlockSpec((1, MEM_LEN, w), lambda bi, i: (bi, 0, 0)),
                  pl.BlockSpec((1, MEM_LEN, w), lambda bi, i: (bi, 0, 0))],
        out_specs=pl.BlockSpec((1, tq, w), lambda bi, i: (bi, i, 0)),
        out_shape=jax.ShapeDtypeStruct((b, s, w), BF16),
        compiler_params=_cparams("parallel", "parallel"),
        name="cross_attention",
    )(q3, mk, mv)


def _xattn_decode_kernel(q_ref, k_ref, v_ref, o_ref, qm_ref):
    scale = X_DH ** -0.5
    t = q_ref.shape[1]
    q = q_ref[0]
    for h in range(X_HEADS):
        qm_ref[h * t:(h + 1) * t, :] = q[:, h * X_DH:(h + 1) * X_DH]
    s = _dot_nt(qm_ref[...].astype(BF16), k_ref[0].astype(BF16)) * scale
    row = lax.broadcasted_iota(jnp.int32, s.shape, 0)
    col = lax.broadcasted_iota(jnp.int32, s.shape, 1)
    same_head = jnp.bitwise_and(col, X_HEADS - 1) == lax.shift_right_logical(row, t.bit_length() - 1)
    s = jnp.where(same_head, s, NEG_INF)
    s = s - jnp.max(s, axis=-1, keepdims=True)
    p = jnp.exp(s)
    p = p / jnp.sum(p, axis=-1, keepdims=True)
    o = _dot(p.astype(BF16), v_ref[0].astype(BF16))
    for h in range(X_HEADS):
        o_ref[0, :, h * X_DH:(h + 1) * X_DH] = o[h * t:(h + 1) * t].astype(o_ref.dtype)


def _cross_attention_decode(q3, mem_k, mem_v):
    b, t, w = q3.shape
    assert t & (t - 1) == 0 and X_HEADS & (X_HEADS - 1) == 0
    rows = MEM_LEN * X_HEADS
    mk = mem_k.reshape(b, rows, X_DH)
    mv = mem_v.reshape(b, rows, X_DH)
    return pl.pallas_call(
        _xattn_decode_kernel,
        grid=(b,),
        in_specs=[pl.BlockSpec((1, t, w), lambda bi: (bi, 0, 0)),
                  pl.BlockSpec((1, rows, X_DH), lambda bi: (bi, 0, 0)),
                  pl.BlockSpec((1, rows, X_DH), lambda bi: (bi, 0, 0))],
        out_specs=pl.BlockSpec((1, t, w), lambda bi: (bi, 0, 0)),
        out_shape=jax.ShapeDtypeStruct((b, t, w), BF16),
        scratch_shapes=[pltpu.VMEM((X_HEADS * t, X_DH), F32)],
        compiler_params=_cparams("parallel"),
        name="cross_attention_decode",
    )(q3, mk, mv)


def _topk_rows(s_ref, work_ref, rank_ref, sorted_ref, *, break_ties):
    shape = s_ref.shape
    nk = shape[0]
    iota = lax.broadcasted_iota(jnp.int32, shape, 0).astype(F32)
    work_ref[...] = s_ref[...]
    rank_ref[...] = jnp.full(shape, float(PEER_TOPK), F32)
    for r in range(PEER_TOPK):
        w = work_ref[...]
        m = jnp.max(w, axis=0, keepdims=True)
        sel = w == m
        if break_ties:
            sel = iota == jnp.min(jnp.where(sel, iota, float(nk)), axis=0, keepdims=True)
        rank_ref[...] = jnp.where(sel, float(r), rank_ref[...])
        work_ref[...] = jnp.where(sel, NEG_INF, w)
        sorted_ref[r:r + 1, :] = m


def _topk_exact(s_ref, work_ref, rank_ref, sorted_ref):
    _topk_rows(s_ref, work_ref, rank_ref, sorted_ref, break_ties=False)
    taken = jnp.sum(jnp.where(rank_ref[...] < PEER_TOPK, 1.0, 0.0), axis=0, keepdims=True)
    tied = jnp.max(jnp.where(taken == float(PEER_TOPK), 0.0, 1.0)) > 0.0

    @pl.when(tied)
    def _():
        _topk_rows(s_ref, work_ref, rank_ref, sorted_ref, break_ties=True)


def _route_kernel(q_ref, keys_ref, e1_ref, n1_ref, e2_ref, r2_ref,
                  st_ref, work_ref, rank1_ref, rank2_ref, s1s_ref, s2s_ref):
    q = q_ref[...]
    for c in range(2):
        qc = q[:, c * PEER_HALF:(c + 1) * PEER_HALF].astype(BF16)
        st_ref[c] = _dot_nt(keys_ref[c, 0].astype(BF16), qc)
    _topk_exact(st_ref.at[0], work_ref, rank1_ref, s1s_ref)
    _topk_exact(st_ref.at[1], work_ref, rank2_ref, s2s_ref)
    st = (st_ref[0], st_ref[1])
    s1 = s1s_ref[...]
    s2 = s2s_ref[...]

    iota = lax.broadcasted_iota(jnp.int32, s1.shape, 0).astype(F32)
    ptr = jnp.zeros(s1.shape, F32)
    count = jnp.zeros(s1.shape, F32)
    front = s1 + s2[0:1, :]
    zsum = jnp.zeros((1, s1.shape[1]), F32)
    top0 = None
    for kk in range(PEER_TOPK):
        m = jnp.max(front, axis=0, keepdims=True)
        if kk == 0:
            top0 = m
        zsum = zsum + jnp.exp(m - top0)
        istar = jnp.min(jnp.where(front == m, iota, float(PEER_TOPK)), axis=0, keepdims=True)
        oh = iota == istar
        count = count + jnp.where(oh, 1.0, 0.0)
        pnew = jnp.sum(jnp.where(oh, ptr, 0.0), axis=0, keepdims=True) + 1.0
        ptr = jnp.where(oh, pnew, ptr)
        s2n = jnp.max(jnp.where(iota == pnew, s2, NEG_INF), axis=0, keepdims=True)
        s1sel = jnp.max(jnp.where(oh, s1, NEG_INF), axis=0, keepdims=True)
        front = jnp.where(oh, s1sel + s2n, front)

    rank1 = rank1_ref[...]
    rank2 = rank2_ref[...]
    n1 = jnp.zeros(rank1.shape, F32)
    for r in range(PEER_TOPK):
        n1 = jnp.where(rank1 == float(r), count[r:r + 1, :], n1)
    inv_z = 1.0 / zsum
    e1_ref[0] = jnp.where(rank1 < PEER_TOPK, jnp.exp(st[0] - s1[0:1, :]) * inv_z, 0.0)
    e2_ref[0] = jnp.where(rank2 < PEER_TOPK, jnp.exp(st[1] - s2[0:1, :]), 0.0).astype(e2_ref.dtype)
    n1_ref[0] = n1
    r2_ref[0] = rank2.astype(r2_ref.dtype)


def _peer_route(qp, peer_keys, *, tt=1024):
    t = qp.shape[0]
    tt = min(tt, t)
    hw = 2 * PEER_HALF
    wide = jax.ShapeDtypeStruct((PEER_HEADS, PEER_NKEYS, t), F32)
    narrow = jax.ShapeDtypeStruct((PEER_HEADS, PEER_NKEYS, t), BF16)
    ospec = pl.BlockSpec((1, PEER_NKEYS, tt), lambda i, h: (h, 0, i))
    return pl.pallas_call(
        _route_kernel,
        grid=(t // tt, PEER_HEADS),
        in_specs=[pl.BlockSpec((tt, hw), lambda i, h: (i, h)),
                  pl.BlockSpec((2, 1, PEER_NKEYS, PEER_HALF), lambda i, h: (0, h, 0, 0))],
        out_specs=[ospec, ospec, ospec, ospec],
        out_shape=[wide, wide, narrow, narrow],
        scratch_shapes=[pltpu.VMEM((2, PEER_NKEYS, tt), F32),
                        pltpu.VMEM((PEER_NKEYS, tt), F32), pltpu.VMEM((PEER_NKEYS, tt), F32),
                        pltpu.VMEM((PEER_NKEYS, tt), F32),
                        pltpu.VMEM((PEER_TOPK, tt), F32), pltpu.VMEM((PEER_TOPK, tt), F32)],
        compiler_params=_cparams("parallel", "parallel"),
        name="peer_route",
    )(qp, peer_keys)


def _transpose_tile_kernel(x_ref, o_ref):
    o_ref[0] = x_ref[...].T.astype(o_ref.dtype)


def _transposed_tiles(x, rows):
    n, d = x.shape
    return pl.pallas_call(
        _transpose_tile_kernel,
        grid=(n // rows,),
        in_specs=[pl.BlockSpec((rows, d), lambda i: (i, 0))],
        out_specs=pl.BlockSpec((1, d, rows), lambda i: (i, 0, 0)),
        out_shape=jax.ShapeDtypeStruct((n // rows, d, rows), BF16),
        compiler_params=_cparams("parallel"),
        name="transpose_tiles",
    )(x)


def _peer_dense_kernel(h2_ref, gf_ref, gl_ref, u_ref, vt_ref, e1_ref, n1_ref, e2_ref, r2_ref, y_ref,
                       xn_ref, acc_ref, ht0_ref, ht1_ref, *, et, n_tiles, th):
    s = pl.program_id(1)
    slabs = et // PEER_NKEYS
    tt = xn_ref.shape[0]
    ht_refs = (ht0_ref, ht1_ref)

    def project(par):
        ht_refs[par][...] = _dot_nt(u_ref[...].astype(BF16), xn_ref[...])

    def gates(tile, c):
        cols = slice(c * th, (c + 1) * th)
        ws = []
        for sl in range(slabs):
            a = tile * slabs + sl
            w = None
            for h in range(PEER_HEADS):
                e1row = e1_ref[h, pl.ds(a, 1), cols].astype(BF16)
                n1row = n1_ref[h, pl.ds(a, 1), cols].astype(BF16)
                term = jnp.where(r2_ref[h, :, cols] < n1row, e2_ref[h, :, cols], 0.0) * e1row
                w = term if w is None else w + term
            ws.append(w)
        return ws[0] if slabs == 1 else jnp.concatenate(ws, axis=0)

    def finish(par):
        for c in range(tt // th):
            cols = slice(c * th, (c + 1) * th)
            act = jax.nn.gelu(ht_refs[par][:, cols]).astype(BF16)
            acc_ref[:, cols] += _dot(vt_ref[0], act * gates(s - 1, c))

    @pl.when(s == 0)
    def _():
        x = h2_ref[...]
        ms = jnp.mean(x * x, axis=-1, keepdims=True)
        xn_ref[...] = (x * lax.rsqrt(ms + EPS) * gf_ref[...]).astype(BF16)
        acc_ref[...] = jnp.zeros_like(acc_ref)
        project(0)

    for par in (0, 1):
        @pl.when(jnp.logical_and(jnp.logical_and(s > 0, s < n_tiles), lax.rem(s, 2) == par))
        def _():
            project(par)
            finish(1 - par)

    @pl.when(s == n_tiles)
    def _():
        finish((n_tiles - 1) % 2)
        r = h2_ref[...] + acc_ref[...].T
        ms = jnp.mean(r * r, axis=-1, keepdims=True)
        y_ref[...] = r * lax.rsqrt(ms + EPS) * gl_ref[...]


def _peer_dense(h2, g_ffn, g_final, u, vt, route, *, tt=512, th=256):
    t = h2.shape[0]
    tt = min(tt, t)
    n_tiles, _, et = vt.shape
    rspec = pl.BlockSpec((PEER_HEADS, PEER_NKEYS, tt), lambda i, s: (0, 0, i))
    return pl.pallas_call(
        functools.partial(_peer_dense_kernel, et=et, n_tiles=n_tiles, th=min(th, tt)),
        grid=(t // tt, n_tiles + 1),
        in_specs=[pl.BlockSpec((tt, D_MODEL), lambda i, s: (i, 0)),
                  pl.BlockSpec((1, D_MODEL), lambda i, s: (0, 0)),
                  pl.BlockSpec((1, D_MODEL), lambda i, s: (0, 0)),
                  pl.BlockSpec((et, D_MODEL), lambda i, s: (jnp.minimum(s, n_tiles - 1), 0)),
                  pl.BlockSpec((1, D_MODEL, et), lambda i, s: (jnp.maximum(s - 1, 0), 0, 0)),
                  rspec, rspec, rspec, rspec],
        out_specs=pl.BlockSpec((tt, D_MODEL), lambda i, s: (i, 0)),
        out_shape=jax.ShapeDtypeStruct((t, D_MODEL), F32),
        scratch_shapes=[pltpu.VMEM((tt, D_MODEL), BF16), pltpu.VMEM((D_MODEL, tt), F32),
                        pltpu.VMEM((et, tt), F32), pltpu.VMEM((et, tt), F32)],
        compiler_params=_cparams("parallel", "arbitrary", vmem=PEER_VMEM_LIMIT),
        name="peer_dense",
    )(h2, g_ffn.reshape(1, D_MODEL), g_final.reshape(1, D_MODEL), u, vt, *route)


def _tail(h1, mem_k, mem_v, w, batch):
    t = h1.shape[0]
    qx = _matmul(h1, w["xq"], gain=w["g_cross"], name="xq_proj")
    q3 = qx.reshape(batch, t // batch, X_HEADS * X_DH)
    attend = _cross_attention_decode if t // batch < SUBLANES else _cross_attention
    ox = attend(q3, mem_k, mem_v)
    h2 = _matmul(ox.reshape(t, X_HEADS * X_DH), w["xo"], residual=h1, name="xo_proj", bn=D_MODEL)
    qp = _matmul(h2, w["pq"], gain=w["g_ffn"], name="peer_query", bn=1024)
    route = _peer_route(qp, w["peer_keys"])
    return _peer_dense(h2, w["g_ffn"], w["g_final"], w["peer_u"], w["peer_vt"], route)


def _mixer_out(x2, z2, ret_o, diff_o, w):
    m = _mixer_gate(z2, ret_o, diff_o, w["ret_norm_g"], w["diff_norm_g"], w["branch_a"], w["branch_b"])
    return _matmul(m, w["out"], residual=x2, name="out_proj", bn=1024)


def kernel(x_prompt, x_sample, cache_k, cache_v, state_ret, cache_mem_k, cache_mem_v, page_table, mem_prompt, g_mix, w_in, ret_norm_g, diff_norm_g, lambda_q1, lambda_k1, lambda_q2, lambda_k2, w_branch_a, w_branch_b, w_out, g_cross, w_xq, w_mem_kv, w_xo, g_ffn, w_pq, peer_keys, peer_u, peer_v, g_final):
    bp, sp, d = x_prompt.shape
    bd, td, _ = x_sample.shape
    w = dict(
        ret_norm_g=ret_norm_g, diff_norm_g=diff_norm_g, g_cross=g_cross, g_ffn=g_ffn, g_final=g_final,
        branch_a=w_branch_a.astype(BF16), branch_b=w_branch_b.astype(BF16), out=w_out.astype(BF16),
        xq=w_xq.astype(BF16), xo=w_xo.astype(BF16), pq=w_pq.astype(BF16), peer_keys=peer_keys,
        peer_u=peer_u, peer_vt=_transposed_tiles(peer_v, PEER_EXPERT_TILE))
    w_in_b = w_in.astype(BF16)
    lams = [a.reshape(1, DIFF_DH).astype(F32) for a in (lambda_q1, lambda_k1, lambda_q2, lambda_k2)]

    xp2 = x_prompt.reshape(bp * sp, d)
    z2, kp, vp = _matmul(xp2, w_in_b, gain=g_mix, name="in_proj", head_major=(OFF_DK, OFF_DV))
    z3 = z2.reshape(bp, sp, IN_W)
    ret_o, ret_state_prompt = _retention(z3, jnp.arange(sp, dtype=jnp.int32), None, RET_CHUNK, "retention_prompt")
    diff_o = _diff_prompt(z3, lams)
    h1 = _mixer_out(xp2, z2, ret_o.reshape(bp * sp, RET_W), diff_o.reshape(bp * sp, DIFF_W), w)
    mem_kv = _matmul(mem_prompt.reshape(bp * MEM_LEN, d), w_mem_kv.astype(BF16), name="mem_kv_proj")
    mem_kv = mem_kv.reshape(bp, MEM_LEN, 2, X_HEADS, X_DH)
    mem_k_prompt, mem_v_prompt = mem_kv[:, :, 0], mem_kv[:, :, 1]
    y_prompt = _tail(h1, mem_k_prompt, mem_v_prompt, w, bp).reshape(bp, sp, d)
    k_prompt, v_prompt = _heads_view(kp, bp), _heads_view(vp, bp)

    past = page_table.shape[1] * PAGE_SIZE
    xs2 = x_sample.reshape(bd * td, d)
    zs2, ks, vs = _matmul(xs2, w_in_b, gain=g_mix, name="in_proj", head_major=(OFF_DK, OFF_DV))
    zs3 = zs2.reshape(bd, td, IN_W)
    ret_o_s, ret_state_sample = _retention(zs3, past + jnp.arange(td, dtype=jnp.int32), state_ret, td,
                                           "retention_sample")
    diff_o_s = _diff_sample(zs3, cache_k, cache_v, page_table, lams)
    h1s = _mixer_out(xs2, zs2, ret_o_s.reshape(bd * td, RET_W), diff_o_s.reshape(bd * td, DIFF_W), w)
    y_sample = _tail(h1s, cache_mem_k, cache_mem_v, w, bd).reshape(bd, td, d)
    k_sample, v_sample = _heads_view(ks, bd), _heads_view(vs, bd)

    return (y_prompt, y_sample, k_prompt, v_prompt, ret_state_prompt, mem_k_prompt, mem_v_prompt,
            k_sample, v_sample, ret_state_sample)
```

```python
import functools
import math

import numpy as np
import jax
import jax.numpy as jnp
from jax import lax
from jax.experimental import pallas as pl
from jax.experimental.pallas import tpu as pltpu

F32 = jnp.float32
BF16 = jnp.bfloat16

D_MODEL = 2048
RET_HEADS = 8
RET_DK = 128
RET_DV = 128
RET_CHUNK = 128
DIFF_HEADS = 4
DIFF_DH = 128
DIFF_HW = 2 * DIFF_DH
PAGE_SIZE = 128
MEM_LEN = 256
X_HEADS = 4
X_DH = 128
PEER_HEADS = 8
PEER_NKEYS = 128
PEER_HALF = 128
PEER_TOPK = 16
ROPE_BASE = 10000.0
LAMBDA_INIT = 0.8 - 0.6 * math.exp(-0.3 * 0)
EPS = 1e-6

RET_W = RET_HEADS * RET_DK
DIFF_W = DIFF_HEADS * DIFF_HW
OFF_RQ, OFF_RK, OFF_RV, OFF_RG = 0, RET_W, 2 * RET_W, 3 * RET_W
OFF_DQ = 4 * RET_W
OFF_DK = OFF_DQ + DIFF_W
OFF_DV = OFF_DK + DIFF_W
OFF_GA = OFF_DV + DIFF_W
OFF_GB = OFF_GA + D_MODEL
IN_W = OFF_GB + D_MODEL

LANES = 128
SUBLANES = 8
VMEM_LIMIT = 48 * 1024 * 1024
PEER_VMEM_LIMIT = 58 * 1024 * 1024
PEER_EXPERT_TILE = 512
SAMPLE_PAGES_PER_STEP = 16

NEG_INF = float("-inf")


def _cparams(*sem, vmem=VMEM_LIMIT, flags=None):
    return pltpu.CompilerParams(dimension_semantics=sem, vmem_limit_bytes=vmem, flags=flags)


def _dot(a, b):
    return jnp.dot(a, b, preferred_element_type=F32)


def _dot_nt(a, b):
    return lax.dot_general(a, b, (((1,), (1,)), ((), ())), preferred_element_type=F32)


def _dot_tn(a, b):
    return lax.dot_general(a, b, (((0,), (0,)), ((), ())), preferred_element_type=F32)


def _mm_kernel(*refs, norm, residual, head_copies):
    it = iter(refs)
    x_ref = next(it)
    g_ref = next(it) if norm else None
    w_ref = next(it)
    r_ref = next(it) if residual else None
    o_ref = next(it)
    copy_refs = [next(it) for _ in head_copies]
    xs_ref = next(it)
    j = pl.program_id(1)

    @pl.when(j == 0)
    def _():
        x = x_ref[...].astype(F32)
        if norm:
            ms = jnp.mean(x * x, axis=-1, keepdims=True)
            x = x * lax.rsqrt(ms + EPS) * g_ref[...]
        xs_ref[...] = x.astype(BF16)

    acc = _dot(xs_ref[...], w_ref[...])
    if residual:
        acc = r_ref[...] + acc
    o_ref[...] = acc.astype(o_ref.dtype)
    chunks = acc.shape[1] // LANES
    for (first, count), c_ref in zip(head_copies, copy_refs):
        for q in range(count):
            @pl.when(j == first + q)
            def _():
                for p in range(chunks):
                    head, half = divmod(q * chunks + p, 2)
                    c_ref[:, half * DIFF_HEADS + head, :] = acc[:, p * LANES:(p + 1) * LANES]


def _matmul(x, w, *, name, gain=None, residual=None, out_dtype=F32, bm=1024, bn=512, head_major=()):
    m, k = x.shape
    n = w.shape[1]
    bm = min(bm, m)
    bn = min(bn, n)
    assert m % bm == 0 and n % bn == 0
    norm = gain is not None
    res = residual is not None
    head_copies = []
    out_specs = [pl.BlockSpec((bm, bn), lambda i, j: (i, j))]
    out_shape = [jax.ShapeDtypeStruct((m, n), out_dtype)]
    for off in head_major:
        assert off % bn == 0 and DIFF_W % bn == 0 and bn % DIFF_HW == 0
        head_copies.append((off // bn, DIFF_W // bn))
        out_specs.append(pl.BlockSpec((bm, 2 * DIFF_HEADS, DIFF_DH), lambda i, j: (i, 0, 0)))
        out_shape.append(jax.ShapeDtypeStruct((m, 2 * DIFF_HEADS, DIFF_DH), F32))
    in_specs = [pl.BlockSpec((bm, k), lambda i, j: (i, 0))]
    args = [x]
    if norm:
        in_specs.append(pl.BlockSpec((1, k), lambda i, j: (0, 0)))
        args.append(gain.reshape(1, k).astype(F32))
    in_specs.append(pl.BlockSpec((k, bn), lambda i, j: (0, j)))
    args.append(w)
    if res:
        in_specs.append(pl.BlockSpec((bm, bn), lambda i, j: (i, j)))
        args.append(residual)
    outs = pl.pallas_call(
        functools.partial(_mm_kernel, norm=norm, residual=res, head_copies=tuple(head_copies)),
        grid=(m // bm, n // bn),
        in_specs=in_specs,
        out_specs=out_specs,
        out_shape=out_shape,
        scratch_shapes=[pltpu.VMEM((bm, k), BF16)],
        compiler_params=_cparams("parallel", "arbitrary"),
        name=name,
    )(*args)
    return outs if head_major else outs[0]


def _heads_view(x, batch):
    s = x.shape[0] // batch
    x5 = x.reshape(batch, s, 2, DIFF_HEADS, DIFF_DH)
    return jnp.transpose(x5, (0, 1, 3, 2, 4)).reshape(batch, s, DIFF_HEADS, DIFF_HW)


def _ret_tables(chunk_len):
    c = RET_CHUNK
    lg = jnp.log1p(-jnp.exp2(-5.0 - jnp.arange(RET_HEADS, dtype=F32)))
    i = jnp.arange(c, dtype=F32)
    rel = i[:, None] - i[None, :]
    causal = rel >= 0
    dmat = jnp.where(causal[None], jnp.exp(jnp.where(causal, rel, 0.0)[None] * lg[:, None, None]), 0.0)
    dec_in = jnp.exp((i + 1.0)[None, :] * lg[:, None])
    dec_out = jnp.exp((chunk_len - 1.0 - i)[None, :] * lg[:, None])
    dec_chunk = jnp.exp(chunk_len * lg)
    ones = jnp.ones((RET_HEADS, c, LANES), F32)
    return (dmat, dec_in[:, :, None] * ones, dec_out[:, :, None] * ones,
            dec_chunk[:, None, None] * ones)


def _rope_tables(pos):
    half = RET_DK // 2
    freqs = jnp.exp(-math.log(ROPE_BASE) * jnp.arange(half, dtype=F32) / half)
    ang = pos.astype(F32)[:, None] * freqs[None, :]
    cos, sin = jnp.cos(ang), jnp.sin(ang)
    return jnp.concatenate([cos, cos], axis=-1), jnp.concatenate([-sin, sin], axis=-1)


def _ret_kernel(*refs, rows, has_state):
    it = iter(refs)
    q_ref, k_ref, v_ref, cos_ref, sin_ref = (next(it) for _ in range(5))
    dmat_ref, din_ref, dout_ref, dch_ref = (next(it) for _ in range(4))
    s0_ref = next(it) if has_state else None
    o_ref, sfin_ref, s_ref = next(it), next(it), next(it)
    pad_ref = next(it) if rows < RET_CHUNK else None
    n = pl.program_id(1)

    @pl.when(n == 0)
    def _():
        if has_state:
            s_ref[...] = s0_ref[0]
        else:
            s_ref[...] = jnp.zeros_like(s_ref)

    def full(x):
        if rows == RET_CHUNK:
            return x
        width = x.shape[1]
        pad_ref[:, 0:width] = jnp.zeros((RET_CHUNK, width), F32)
        pad_ref[0:rows, 0:width] = x
        return pad_ref[:, 0:width]

    def rot(x, cos, sin):
        return x * cos + pltpu.roll(x, RET_DK // 2, axis=1) * sin

    cos, sin = full(cos_ref[...]), full(sin_ref[...])
    q_all, k_all, v_all = full(q_ref[0]), full(k_ref[0]), full(v_ref[0])
    for h in range(RET_HEADS):
        sl = slice(h * RET_DK, (h + 1) * RET_DK)
        q = rot(q_all[:, sl], cos, sin)
        k = rot(k_all[:, sl], cos, sin) * (RET_DK ** -0.5)
        v = v_all[:, sl].astype(BF16)
        s_prev = s_ref[h]
        qb = q.astype(BF16)
        scores = _dot_nt(qb, k.astype(BF16)) * dmat_ref[h]
        out = _dot(scores.astype(BF16), v) + _dot(qb, s_prev.astype(BF16)) * din_ref[h]
        s_ref[h] = dch_ref[h] * s_prev + _dot_tn((k * dout_ref[h]).astype(BF16), v)
        o_ref[0, :, sl] = out if rows == RET_CHUNK else out[0:rows, :]

    @pl.when(n == pl.num_programs(1) - 1)
    def _():
        sfin_ref[0] = s_ref[...]


def _retention(z3, pos, state0, chunk_len, name):
    b, s, _ = z3.shape
    rows = min(s, RET_CHUNK)
    nchunks = s // rows
    cos, sin = _rope_tables(pos)
    dmat, din, dout, dch = _ret_tables(float(chunk_len))
    has_state = state0 is not None
    qkv_spec = lambda off: pl.BlockSpec((1, rows, RET_W), lambda bi, n: (bi, n, off // RET_W))
    tab_spec = pl.BlockSpec((RET_HEADS, RET_CHUNK, LANES), lambda bi, n: (0, 0, 0))
    state_spec = pl.BlockSpec((1, RET_HEADS, RET_DK, RET_DV), lambda bi, n: (bi, 0, 0, 0))
    in_specs = [qkv_spec(OFF_RQ), qkv_spec(OFF_RK), qkv_spec(OFF_RV),
                pl.BlockSpec((rows, LANES), lambda bi, n: (n, 0)),
                pl.BlockSpec((rows, LANES), lambda bi, n: (n, 0)),
                tab_spec, tab_spec, tab_spec, tab_spec]
    args = [z3, z3, z3, cos, sin, dmat, din, dout, dch]
    if has_state:
        in_specs.append(state_spec)
        args.append(state0)
    scratch = [pltpu.VMEM((RET_HEADS, RET_DK, RET_DV), F32)]
    if rows < RET_CHUNK:
        scratch.append(pltpu.VMEM((RET_CHUNK, RET_W), F32))
    return pl.pallas_call(
        functools.partial(_ret_kernel, rows=rows, has_state=has_state),
        grid=(b, nchunks),
        in_specs=in_specs,
        out_specs=[pl.BlockSpec((1, rows, RET_W), lambda bi, n: (bi, n, 0)), state_spec],
        out_shape=[jax.ShapeDtypeStruct((b, s, RET_W), F32),
                   jax.ShapeDtypeStruct((b, RET_HEADS, RET_DK, RET_DV), F32)],
        scratch_shapes=scratch,
        compiler_params=_cparams("parallel", "arbitrary"),
        name=name,
    )(*args)


def _lambda_value(lq1_ref, lk1_ref, lq2_ref, lk2_ref):
    a = jnp.sum(lq1_ref[...] * lk1_ref[...], axis=-1, keepdims=True)
    b = jnp.sum(lq2_ref[...] * lk2_ref[...], axis=-1, keepdims=True)
    return jnp.exp(a) - jnp.exp(b) + LAMBDA_INIT


def _online_update(s, pv, m_ref, l_ref, acc_ref, idx):
    def lanes(x, width):
        return x if width == LANES else jnp.concatenate([x] * (width // LANES), axis=1)

    m_prev = m_ref[idx]
    m_new = jnp.maximum(m_prev, jnp.max(s, axis=-1, keepdims=True))
    alpha = jnp.exp(m_prev - m_new)
    p = jnp.exp(s - lanes(m_new, s.shape[1]))
    l_ref[idx] = alpha * l_ref[idx] + jnp.sum(p, axis=-1, keepdims=True)
    acc = acc_ref[idx]
    acc_ref[idx] = lanes(alpha, acc.shape[1]) * acc + pv(p)
    m_ref[idx] = m_new


def _diff_prompt_kernel(qi_ref, ki_ref, q_ref, k_ref, v_ref, lq1, lk1, lq2, lk2, o_ref,
                        m_ref, l_ref, acc_ref):
    pair = pl.program_id(2)
    qi, ki = qi_ref[pair], ki_ref[pair]
    scale = DIFF_DH ** -0.5

    @pl.when(ki == 0)
    def _():
        m_ref[...] = jnp.full_like(m_ref, NEG_INF)
        l_ref[...] = jnp.zeros_like(l_ref)
        acc_ref[...] = jnp.zeros_like(acc_ref)

    def step(masked):
        q = q_ref[0]
        k = k_ref[0]
        v = v_ref[0].astype(BF16)
        for c in range(2):
            sl = slice(c * DIFF_DH, (c + 1) * DIFF_DH)
            s = _dot_nt(q[:, sl].astype(BF16), k[:, sl].astype(BF16)) * scale
            if masked:
                row = lax.broadcasted_iota(jnp.int32, s.shape, 0)
                col = lax.broadcasted_iota(jnp.int32, s.shape, 1)
                s = jnp.where(row >= col, s, NEG_INF)
            _online_update(s, lambda p: _dot(p.astype(BF16), v), m_ref, l_ref, acc_ref, c)

    @pl.when(ki < qi)
    def _():
        step(False)

    @pl.when(ki == qi)
    def _():
        step(True)
        lam = _lambda_value(lq1, lk1, lq2, lk2)
        o0 = acc_ref[0] / l_ref[0][:, 0:1]
        o1 = acc_ref[1] / l_ref[1][:, 0:1]
        o_ref[0] = o0 - lam * o1


def _diff_prompt(z3, lams, *, tq=512):
    b, s, _ = z3.shape
    nq = s // tq
    pairs = [(qi, ki) for qi in range(nq) for ki in range(qi + 1)]
    qi_tab = jnp.asarray([p[0] for p in pairs], jnp.int32)
    ki_tab = jnp.asarray([p[1] for p in pairs], jnp.int32)
    col = lambda off: off // DIFF_HW
    lam_spec = pl.BlockSpec((1, DIFF_DH), lambda bi, h, p, qt, kt: (0, 0))
    grid_spec = pltpu.PrefetchScalarGridSpec(
        num_scalar_prefetch=2,
        grid=(b, DIFF_HEADS, len(pairs)),
        in_specs=[
            pl.BlockSpec((1, tq, DIFF_HW), lambda bi, h, p, qt, kt: (bi, qt[p], col(OFF_DQ) + h)),
            pl.BlockSpec((1, tq, DIFF_HW), lambda bi, h, p, qt, kt: (bi, kt[p], col(OFF_DK) + h)),
            pl.BlockSpec((1, tq, DIFF_HW), lambda bi, h, p, qt, kt: (bi, kt[p], col(OFF_DV) + h)),
            lam_spec, lam_spec, lam_spec, lam_spec],
        out_specs=pl.BlockSpec((1, tq, DIFF_HW), lambda bi, h, p, qt, kt: (bi, qt[p], h)),
        scratch_shapes=[pltpu.VMEM((2, tq, LANES), F32), pltpu.VMEM((2, tq, LANES), F32),
                        pltpu.VMEM((2, tq, DIFF_HW), F32)])
    return pl.pallas_call(
        _diff_prompt_kernel,
        grid_spec=grid_spec,
        out_shape=jax.ShapeDtypeStruct((b, s, DIFF_W), F32),
        compiler_params=_cparams("parallel", "parallel", "arbitrary"),
        name="diff_prompt",
    )(qi_tab, ki_tab, z3, z3, z3, *lams)


def _diff_sample_kernel(pt_ref, q_ref, kn_ref, vn_ref, *rest, t, g):
    k_refs, v_refs = rest[:g], rest[g:2 * g]
    lq1, lk1, lq2, lk2, o_ref, qm_ref, m_ref, l_ref, acc_ref, pad_ref = rest[2 * g:]
    p = pl.program_id(1)
    scale = DIFF_DH ** -0.5
    hrows = 2 * t
    nrow = DIFF_HEADS * hrows
    slots = 2 * DIFF_HEADS
    prow = PAGE_SIZE * slots
    slot_bits, hrow_bits = slots.bit_length() - 1, hrows.bit_length() - 1

    @pl.when(p == 0)
    def _():
        m_ref[...] = jnp.full_like(m_ref, NEG_INF)
        l_ref[...] = jnp.zeros_like(l_ref)
        acc_ref[...] = jnp.zeros_like(acc_ref)
        q = q_ref[0]
        for h in range(DIFF_HEADS):
            for c in range(2):
                r0 = h * hrows + c * t
                lo = h * DIFF_HW + c * DIFF_DH
                qm_ref[r0:r0 + t, :] = q[:, lo:lo + DIFF_DH]

    qm = qm_ref[...].astype(BF16)
    row = lax.broadcasted_iota(jnp.int32, (nrow, prow), 0)
    col = lax.broadcasted_iota(jnp.int32, (nrow, prow), 1)
    row_half = jnp.where(jnp.bitwise_and(row, hrows - 1) >= t, DIFF_HEADS, 0)
    row_slot = row_half + lax.shift_right_logical(row, hrow_bits)
    match = jnp.bitwise_and(col, slots - 1) == row_slot

    def page_scores(kpage):
        s = _dot_nt(qm, kpage.astype(BF16)) * scale
        return jnp.where(match, s, NEG_INF)

    def page_values(vpage):
        v3 = vpage.reshape(PAGE_SIZE, slots, DIFF_DH)
        other = pltpu.roll(v3, DIFF_HEADS, 1).reshape(prow, DIFF_DH)
        return jnp.concatenate([vpage.astype(BF16), other.astype(BF16)], axis=1)

    def update(scores, values):
        def pv(pr):
            out = None
            for i, vm in enumerate(values):
                o = _dot(pr[:, i * prow:(i + 1) * prow].astype(BF16), vm)
                out = o if out is None else out + o
            return out
        s = scores[0] if len(scores) == 1 else jnp.concatenate(scores, axis=1)
        _online_update(s, pv, m_ref, l_ref, acc_ref, slice(None))

    update([page_scores(k[0]) for k in k_refs], [page_values(v[0]) for v in v_refs])

    @pl.when(p == pl.num_programs(1) - 1)
    def _():
        pad_ref[...] = jnp.zeros_like(pad_ref)
        kn, vn = kn_ref[0], vn_ref[0]
        for tok in range(t):
            for c in range(2):
                for h in range(DIFF_HEADS):
                    r = tok * slots + c * DIFF_HEADS + h
                    lo = h * DIFF_HW + c * DIFF_DH
                    pad_ref[0, r:r + 1, :] = kn[tok:tok + 1, lo:lo + DIFF_DH]
                    pad_ref[1, r:r + 1, :] = vn[tok:tok + 1, lo:lo + DIFF_DH]
        s = page_scores(pad_ref[0])
        causal = lax.shift_right_logical(col, slot_bits) <= jnp.bitwise_and(row, t - 1)
        update([jnp.where(causal, s, NEG_INF)], [page_values(pad_ref[1])])
        lam = _lambda_value(lq1, lk1, lq2, lk2)
        acc = acc_ref[...]
        swapped = jnp.concatenate([acc[:, DIFF_DH:], acc[:, :DIFF_DH]], axis=1)
        out_row = lax.broadcasted_iota(jnp.int32, acc.shape, 0)
        second_half = jnp.bitwise_and(out_row, hrows - 1) >= t
        o = jnp.where(second_half, swapped, acc) / l_ref[...][:, 0:1]
        for h in range(DIFF_HEADS):
            r0 = h * hrows
            o_ref[0, :, h * DIFF_HW:(h + 1) * DIFF_HW] = o[r0:r0 + t] - lam * o[r0 + t:r0 + hrows]


def _page_rows(cache):
    n_pool = cache.shape[0]
    c5 = cache.reshape(n_pool, PAGE_SIZE, DIFF_HEADS, 2, DIFF_DH)
    return jnp.transpose(c5, (0, 1, 3, 2, 4)).reshape(n_pool, PAGE_SIZE * 2 * DIFF_HEADS, DIFF_DH)


def _diff_sample(zs3, cache_k, cache_v, page_table, lams):
    b, t, _ = zs3.shape
    assert t & (t - 1) == 0
    n_pages = page_table.shape[1]
    g = math.gcd(SAMPLE_PAGES_PER_STEP, n_pages)
    nrow = DIFF_HEADS * 2 * t
    prow = PAGE_SIZE * 2 * DIFF_HEADS
    zcol = lambda off: (lambda bi, p, pt: (bi, 0, off // DIFF_W))
    page = lambda i: (lambda bi, p, pt: (pt[bi, p * g + i], 0, 0))
    page_spec = lambda i: pl.BlockSpec((1, prow, DIFF_DH), page(i))
    lam_spec = pl.BlockSpec((1, DIFF_DH), lambda bi, p, pt: (0, 0))
    grid_spec = pltpu.PrefetchScalarGridSpec(
        num_scalar_prefetch=1,
        grid=(b, n_pages // g),
        in_specs=[
            pl.BlockSpec((1, t, DIFF_W), zcol(OFF_DQ)),
            pl.BlockSpec((1, t, DIFF_W), zcol(OFF_DK)),
            pl.BlockSpec((1, t, DIFF_W), zcol(OFF_DV)),
            *[page_spec(i) for i in range(g)],
            *[page_spec(i) for i in range(g)],
            lam_spec, lam_spec, lam_spec, lam_spec],
        out_specs=pl.BlockSpec((1, t, DIFF_W), lambda bi, p, pt: (bi, 0, 0)),
        scratch_shapes=[pltpu.VMEM((nrow, DIFF_DH), F32),
                        pltpu.VMEM((nrow, LANES), F32),
                        pltpu.VMEM((nrow, LANES), F32),
                        pltpu.VMEM((nrow, DIFF_HW), F32),
                        pltpu.VMEM((2, prow, DIFF_DH), F32)])
    ck, cv = _page_rows(cache_k), _page_rows(cache_v)
    return pl.pallas_call(
        functools.partial(_diff_sample_kernel, t=t, g=g),
        grid_spec=grid_spec,
        out_shape=jax.ShapeDtypeStruct((b, t, DIFF_W), F32),
        compiler_params=_cparams("parallel", "arbitrary"),
        name="diff_sample",
    )(page_table, zs3, zs3, zs3, *([ck] * g), *([cv] * g), *lams)


def _mix_kernel(ro_ref, rg_ref, do_ref, ga_ref, gb_ref, gr_ref, gd_ref, wa_ref, wb_ref, m_ref, ry_ref, dy_ref):
    @pl.when(pl.program_id(1) == 0)
    def _():
        for h in range(RET_HEADS):
            sl = slice(h * RET_DV, (h + 1) * RET_DV)
            o = ro_ref[:, sl]
            y = o * lax.rsqrt(jnp.mean(o * o, axis=-1, keepdims=True) + EPS) * gr_ref[:, sl]
            g = rg_ref[:, sl]
            ry_ref[:, sl] = (y * (g * jax.nn.sigmoid(g))).astype(BF16)
        for h in range(DIFF_HEADS):
            sl = slice(h * DIFF_HW, (h + 1) * DIFF_HW)
            o = do_ref[:, sl]
            y = o * lax.rsqrt(jnp.mean(o * o, axis=-1, keepdims=True) + EPS) * gd_ref[...] * (1.0 - LAMBDA_INIT)
            dy_ref[:, sl] = y.astype(BF16)

    pa = _dot(ry_ref[...], wa_ref[...])
    pb = _dot(dy_ref[...], wb_ref[...])
    m = jax.nn.sigmoid(ga_ref[...]) * pa + jax.nn.sigmoid(gb_ref[...]) * pb
    m_ref[...] = m.astype(m_ref.dtype)


def _mixer_gate(z2, ret_o, diff_o, ret_norm_g, diff_norm_g, wa, wb, *, bm=512, bn=1024):
    m = z2.shape[0]
    bm = min(bm, m)
    return pl.pallas_call(
        _mix_kernel,
        grid=(m // bm, D_MODEL // bn),
        in_specs=[
            pl.BlockSpec((bm, RET_W), lambda i, j: (i, 0)),
            pl.BlockSpec((bm, RET_W), lambda i, j: (i, OFF_RG // RET_W)),
            pl.BlockSpec((bm, DIFF_W), lambda i, j: (i, 0)),
            pl.BlockSpec((bm, bn), lambda i, j: (i, OFF_GA // bn + j)),
            pl.BlockSpec((bm, bn), lambda i, j: (i, OFF_GB // bn + j)),
            pl.BlockSpec((1, RET_W), lambda i, j: (0, 0)),
            pl.BlockSpec((1, DIFF_HW), lambda i, j: (0, 0)),
            pl.BlockSpec((RET_W, bn), lambda i, j: (0, j)),
            pl.BlockSpec((DIFF_W, bn), lambda i, j: (0, j))],
        out_specs=pl.BlockSpec((bm, bn), lambda i, j: (i, j)),
        out_shape=jax.ShapeDtypeStruct((m, D_MODEL), BF16),
        scratch_shapes=[pltpu.VMEM((bm, RET_W), BF16), pltpu.VMEM((bm, DIFF_W), BF16)],
        compiler_params=_cparams("parallel", "arbitrary"),
        name="mixer_gate",
    )(ret_o, z2, diff_o, z2, z2, ret_norm_g.reshape(1, RET_W), diff_norm_g.reshape(1, DIFF_HW), wa, wb)


def _xattn_kernel(q_ref, k_ref, v_ref, o_ref):
    scale = X_DH ** -0.5
    q = q_ref[0]
    k = k_ref[0]
    v = v_ref[0]
    for h in range(X_HEADS):
        sl = slice(h * X_DH, (h + 1) * X_DH)
        s = _dot_nt(q[:, sl].astype(BF16), k[:, sl].astype(BF16)) * scale
        s = s - jnp.max(s, axis=-1, keepdims=True)
        p = jnp.exp(s)
        p = p / jnp.sum(p, axis=-1, keepdims=True)
        o_ref[0, :, sl] = _dot(p.astype(BF16), v[:, sl].astype(BF16)).astype(o_ref.dtype)


def _cross_attention(q3, mem_k, mem_v, *, tq=512):
    b, s, w = q3.shape
    tq = min(tq, s)
    mk = mem_k.reshape(b, MEM_LEN, w)
    mv = mem_v.reshape(b, MEM_LEN, w)
    return pl.pallas_call(
        _xattn_kernel,
        grid=(b, s // tq),
        in_specs=[pl.BlockSpec((1, tq, w), lambda bi, i: (bi, i, 0)),
                  pl.BlockSpec((1, MEM_LEN, w), lambda bi, i: (bi, 0, 0)),
                  pl.BlockSpec((1, MEM_LEN, w), lambda bi, i: (bi, 0, 0))],
        out_specs=pl.BlockSpec((1, tq, w), lambda bi, i: (bi, i, 0)),
        out_shape=jax.ShapeDtypeStruct((b, s, w), BF16),
        compiler_params=_cparams("parallel", "parallel"),
        name="cross_attention",
    )(q3, mk, mv)


def _xattn_decode_kernel(q_ref, k_ref, v_ref, o_ref, qm_ref):
    scale = X_DH ** -0.5
    t = q_ref.shape[1]
    q = q_ref[0]
    for h in range(X_HEADS):
        qm_ref[h * t:(h + 1) * t, :] = q[:, h * X_DH:(h + 1) * X_DH]
    s = _dot_nt(qm_ref[...].astype(BF16), k_ref[0].astype(BF16)) * scale
    row = lax.broadcasted_iota(jnp.int32, s.shape, 0)
    col = lax.broadcasted_iota(jnp.int32, s.shape, 1)
    same_head = jnp.bitwise_and(col, X_HEADS - 1) == lax.shift_right_logical(row, t.bit_length() - 1)
    s = jnp.where(same_head, s, NEG_INF)
    s = s - jnp.max(s, axis=-1, keepdims=True)
    p = jnp.exp(s)
    p = p / jnp.sum(p, axis=-1, keepdims=True)
    o = _dot(p.astype(BF16), v_ref[0].astype(BF16))
    for h in range(X_HEADS):
        o_ref[0, :, h * X_DH:(h + 1) * X_DH] = o[h * t:(h + 1) * t].astype(o_ref.dtype)


def _cross_attention_decode(q3, mem_k, mem_v):
    b, t, w = q3.shape
    assert t & (t - 1) == 0 and X_HEADS & (X_HEADS - 1) == 0
    rows = MEM_LEN * X_HEADS
    mk = mem_k.reshape(b, rows, X_DH)
    mv = mem_v.reshape(b, rows, X_DH)
    return pl.pallas_call(
        _xattn_decode_kernel,
        grid=(b,),
        in_specs=[pl.BlockSpec((1, t, w), lambda bi: (bi, 0, 0)),
                  pl.BlockSpec((1, rows, X_DH), lambda bi: (bi, 0, 0)),
                  pl.BlockSpec((1, rows, X_DH), lambda bi: (bi, 0, 0))],
        out_specs=pl.BlockSpec((1, t, w), lambda bi: (bi, 0, 0)),
        out_shape=jax.ShapeDtypeStruct((b, t, w), BF16),
        scratch_shapes=[pltpu.VMEM((X_HEADS * t, X_DH), F32)],
        compiler_params=_cparams("parallel"),
        name="cross_attention_decode",
    )(q3, mk, mv)


def _topk_rows(s_ref, work_ref, rank_ref, sorted_ref, *, break_ties):
    nk, tt = s_ref.shape
    iota = lax.broadcasted_iota(jnp.int32, (nk, LANES), 0).astype(F32)
    for c0 in range(0, tt, LANES):
        cols = slice(c0, c0 + LANES)
        w = s_ref[:, cols]
        rank = jnp.full((nk, LANES), float(PEER_TOPK), F32)
        for r in range(PEER_TOPK):
            m = jnp.max(w, axis=0, keepdims=True)
            sel = w == m
            if break_ties:
                sel = iota == jnp.min(jnp.where(sel, iota, float(nk)), axis=0, keepdims=True)
            rank = jnp.where(sel, float(r), rank)
            w = jnp.where(sel, NEG_INF, w)
            sorted_ref[r:r + 1, cols] = m
        rank_ref[:, cols] = rank


def _topk_exact(s_ref, work_ref, rank_ref, sorted_ref):
    _topk_rows(s_ref, work_ref, rank_ref, sorted_ref, break_ties=False)
    taken = jnp.sum(jnp.where(rank_ref[...] < PEER_TOPK, 1.0, 0.0), axis=0, keepdims=True)
    tied = jnp.max(jnp.where(taken == float(PEER_TOPK), 0.0, 1.0)) > 0.0

    @pl.when(tied)
    def _():
        _topk_rows(s_ref, work_ref, rank_ref, sorted_ref, break_ties=True)


def _route_kernel(q_ref, keys_ref, e1_ref, n1_ref, e2_ref, r2_ref,
                  st_ref, work_ref, rank1_ref, rank2_ref, s1s_ref, s2s_ref):
    q = q_ref[...]
    for c in range(2):
        qc = q[:, c * PEER_HALF:(c + 1) * PEER_HALF].astype(BF16)
        st_ref[c] = _dot_nt(keys_ref[c, 0].astype(BF16), qc)
    _topk_exact(st_ref.at[0], work_ref, rank1_ref, s1s_ref)
    _topk_exact(st_ref.at[1], work_ref, rank2_ref, s2s_ref)
    st = (st_ref[0], st_ref[1])
    s1 = s1s_ref[...]
    s2 = s2s_ref[...]

    iota = lax.broadcasted_iota(jnp.int32, s1.shape, 0).astype(F32)
    ptr = jnp.zeros(s1.shape, F32)
    count = jnp.zeros(s1.shape, F32)
    front = s1 + s2[0:1, :]
    zsum = jnp.zeros((1, s1.shape[1]), F32)
    top0 = None
    for kk in range(PEER_TOPK):
        m = jnp.max(front, axis=0, keepdims=True)
        if kk == 0:
            top0 = m
        zsum = zsum + jnp.exp(m - top0)
        istar = jnp.min(jnp.where(front == m, iota, float(PEER_TOPK)), axis=0, keepdims=True)
        oh = iota == istar
        count = count + jnp.where(oh, 1.0, 0.0)
        pnew = jnp.sum(jnp.where(oh, ptr, 0.0), axis=0, keepdims=True) + 1.0
        ptr = jnp.where(oh, pnew, ptr)
        s2n = jnp.max(jnp.where(iota == pnew, s2, NEG_INF), axis=0, keepdims=True)
        s1sel = jnp.max(jnp.where(oh, s1, NEG_INF), axis=0, keepdims=True)
        front = jnp.where(oh, s1sel + s2n, front)

    rank1 = rank1_ref[...]
    rank2 = rank2_ref[...]
    n1 = jnp.zeros(rank1.shape, F32)
    for r in range(PEER_TOPK):
        n1 = jnp.where(rank1 == float(r), count[r:r + 1, :], n1)
    inv_z = 1.0 / zsum
    e1_ref[0] = jnp.where(rank1 < PEER_TOPK, jnp.exp(st[0] - s1[0:1, :]) * inv_z, 0.0)
    e2_ref[0] = jnp.where(rank2 < PEER_TOPK, jnp.exp(st[1] - s2[0:1, :]), 0.0).astype(e2_ref.dtype)
    n1_ref[0] = n1
    r2_ref[0] = rank2.astype(r2_ref.dtype)


def _peer_route(qp, peer_keys, *, tt=1024):
    t = qp.shape[0]
    tt = min(tt, t)
    hw = 2 * PEER_HALF
    wide = jax.ShapeDtypeStruct((PEER_HEADS, PEER_NKEYS, t), F32)
    narrow = jax.ShapeDtypeStruct((PEER_HEADS, PEER_NKEYS, t), BF16)
    ospec = pl.BlockSpec((1, PEER_NKEYS, tt), lambda i, h: (h, 0, i))
    return pl.pallas_call(
        _route_kernel,
        grid=(t // tt, PEER_HEADS),
        in_specs=[pl.BlockSpec((tt, hw), lambda i, h: (i, h)),
                  pl.BlockSpec((2, 1, PEER_NKEYS, PEER_HALF), lambda i, h: (0, h, 0, 0))],
        out_specs=[ospec, ospec, ospec, ospec],
        out_shape=[wide, wide, narrow, narrow],
        scratch_shapes=[pltpu.VMEM((2, PEER_NKEYS, tt), F32),
                        pltpu.VMEM((PEER_NKEYS, tt), F32), pltpu.VMEM((PEER_NKEYS, tt), F32),
                        pltpu.VMEM((PEER_NKEYS, tt), F32),
                        pltpu.VMEM((PEER_TOPK, tt), F32), pltpu.VMEM((PEER_TOPK, tt), F32)],
        compiler_params=_cparams("parallel", "parallel"),
        name="peer_route",
    )(qp, peer_keys)


def _transpose_tile_kernel(x_ref, o_ref):
    o_ref[0] = x_ref[...].T.astype(o_ref.dtype)


def _transposed_tiles(x, rows):
    n, d = x.shape
    return pl.pallas_call(
        _transpose_tile_kernel,
        grid=(n // rows,),
        in_specs=[pl.BlockSpec((rows, d), lambda i: (i, 0))],
        out_specs=pl.BlockSpec((1, d, rows), lambda i: (i, 0, 0)),
        out_shape=jax.ShapeDtypeStruct((n // rows, d, rows), BF16),
        compiler_params=_cparams("parallel"),
        name="transpose_tiles",
    )(x)


def _peer_dense_kernel(h2_ref, gf_ref, gl_ref, u_ref, vt_ref, e1_ref, n1_ref, e2_ref, r2_ref, y_ref,
                       xn_ref, acc_ref, ht0_ref, ht1_ref, *, et, n_tiles, th):
    s = pl.program_id(1)
    slabs = et // PEER_NKEYS
    tt = xn_ref.shape[0]
    ht_refs = (ht0_ref, ht1_ref)

    def project(par):
        ht_refs[par][...] = _dot_nt(u_ref[...].astype(BF16), xn_ref[...])

    def gates(tile, c):
        cols = slice(c * th, (c + 1) * th)
        ws = []
        for sl in range(slabs):
            a = tile * slabs + sl
            w = None
            for h in range(PEER_HEADS):
                e1row = e1_ref[h, pl.ds(a, 1), cols].astype(BF16)
                n1row = n1_ref[h, pl.ds(a, 1), cols].astype(BF16)
                term = jnp.where(r2_ref[h, :, cols] < n1row, e2_ref[h, :, cols], 0.0) * e1row
                w = term if w is None else w + term
            ws.append(w)
        return ws[0] if slabs == 1 else jnp.concatenate(ws, axis=0)

    def finish(par):
        for c in range(tt // th):
            cols = slice(c * th, (c + 1) * th)
            act = jax.nn.gelu(ht_refs[par][:, cols]).astype(BF16)
            acc_ref[:, cols] += _dot(vt_ref[0], act * gates(s - 1, c))

    @pl.when(s == 0)
    def _():
        x = h2_ref[...]
        ms = jnp.mean(x * x, axis=-1, keepdims=True)
        xn_ref[...] = (x * lax.rsqrt(ms + EPS) * gf_ref[...]).astype(BF16)
        acc_ref[...] = jnp.zeros_like(acc_ref)
        project(0)

    for par in (0, 1):
        @pl.when(jnp.logical_and(jnp.logical_and(s > 0, s < n_tiles), lax.rem(s, 2) == par))
        def _():
            project(par)
            finish(1 - par)

    @pl.when(s == n_tiles)
    def _():
        finish((n_tiles - 1) % 2)
        r = h2_ref[...] + acc_ref[...].T
        ms = jnp.mean(r * r, axis=-1, keepdims=True)
        y_ref[...] = r * lax.rsqrt(ms + EPS) * gl_ref[...]


def _peer_dense(h2, g_ffn, g_final, u, vt, route, *, tt=512, th=256):
    t = h2.shape[0]
    tt = min(tt, t)
    n_tiles, _, et = vt.shape
    rspec = pl.BlockSpec((PEER_HEADS, PEER_NKEYS, tt), lambda i, s: (0, 0, i))
    return pl.pallas_call(
        functools.partial(_peer_dense_kernel, et=et, n_tiles=n_tiles, th=min(th, tt)),
        grid=(t // tt, n_tiles + 1),
        in_specs=[pl.BlockSpec((tt, D_MODEL), lambda i, s: (i, 0)),
                  pl.BlockSpec((1, D_MODEL), lambda i, s: (0, 0)),
                  pl.BlockSpec((1, D_MODEL), lambda i, s: (0, 0)),
                  pl.BlockSpec((et, D_MODEL), lambda i, s: (jnp.minimum(s, n_tiles - 1), 0)),
                  pl.BlockSpec((1, D_MODEL, et), lambda i, s: (jnp.maximum(s - 1, 0), 0, 0)),
                  rspec, rspec, rspec, rspec],
        out_specs=pl.BlockSpec((tt, D_MODEL), lambda i, s: (i, 0)),
        out_shape=jax.ShapeDtypeStruct((t, D_MODEL), F32),
        scratch_shapes=[pltpu.VMEM((tt, D_MODEL), BF16), pltpu.VMEM((D_MODEL, tt), F32),
                        pltpu.VMEM((et, tt), F32), pltpu.VMEM((et, tt), F32)],
        compiler_params=_cparams("parallel", "arbitrary", vmem=PEER_VMEM_LIMIT),
        name="peer_dense",
    )(h2, g_ffn.reshape(1, D_MODEL), g_final.reshape(1, D_MODEL), u, vt, *route)


def _tail(h1, mem_k, mem_v, w, batch):
    t = h1.shape[0]
    qx = _matmul(h1, w["xq"], gain=w["g_cross"], name="xq_proj")
    q3 = qx.reshape(batch, t // batch, X_HEADS * X_DH)
    attend = _cross_attention_decode if t // batch < SUBLANES else _cross_attention
    ox = attend(q3, mem_k, mem_v)
    h2 = _matmul(ox.reshape(t, X_HEADS * X_DH), w["xo"], residual=h1, name="xo_proj", bn=D_MODEL)
    qp = _matmul(h2, w["pq"], gain=w["g_ffn"], name="peer_query", bn=1024)
    route = _peer_route(qp, w["peer_keys"])
    return _peer_dense(h2, w["g_ffn"], w["g_final"], w["peer_u"], w["peer_vt"], route)


def _mixer_out(x2, z2, ret_o, diff_o, w):
    m = _mixer_gate(z2, ret_o, diff_o, w["ret_norm_g"], w["diff_norm_g"], w["branch_a"], w["branch_b"])
    return _matmul(m, w["out"], residual=x2, name="out_proj", bn=1024)


def kernel(x_prompt, x_sample, cache_k, cache_v, state_ret, cache_mem_k, cache_mem_v, page_table, mem_prompt, g_mix, w_in, ret_norm_g, diff_norm_g, lambda_q1, lambda_k1, lambda_q2, lambda_k2, w_branch_a, w_branch_b, w_out, g_cross, w_xq, w_mem_kv, w_xo, g_ffn, w_pq, peer_keys, peer_u, peer_v, g_final):
    bp, sp, d = x_prompt.shape
    bd, td, _ = x_sample.shape
    w = dict(
        ret_norm_g=ret_norm_g, diff_norm_g=diff_norm_g, g_cross=g_cross, g_ffn=g_ffn, g_final=g_final,
        branch_a=w_branch_a.astype(BF16), branch_b=w_branch_b.astype(BF16), out=w_out.astype(BF16),
        xq=w_xq.astype(BF16), xo=w_xo.astype(BF16), pq=w_pq.astype(BF16), peer_keys=peer_keys,
        peer_u=peer_u, peer_vt=_transposed_tiles(peer_v, PEER_EXPERT_TILE))
    w_in_b = w_in.astype(BF16)
    lams = [a.reshape(1, DIFF_DH).astype(F32) for a in (lambda_q1, lambda_k1, lambda_q2, lambda_k2)]

    xp2 = x_prompt.reshape(bp * sp, d)
    z2, kp, vp = _matmul(xp2, w_in_b, gain=g_mix, name="in_proj", head_major=(OFF_DK, OFF_DV))
    z3 = z2.reshape(bp, sp, IN_W)
    ret_o, ret_state_prompt = _retention(z3, jnp.arange(sp, dtype=jnp.int32), None, RET_CHUNK, "retention_prompt")
    diff_o = _diff_prompt(z3, lams)
    h1 = _mixer_out(xp2, z2, ret_o.reshape(bp * sp, RET_W), diff_o.reshape(bp * sp, DIFF_W), w)
    mem_kv = _matmul(mem_prompt.reshape(bp * MEM_LEN, d), w_mem_kv.astype(BF16), name="mem_kv_proj")
    mem_kv = mem_kv.reshape(bp, MEM_LEN, 2, X_HEADS, X_DH)
    mem_k_prompt, mem_v_prompt = mem_kv[:, :, 0], mem_kv[:, :, 1]
    y_prompt = _tail(h1, mem_k_prompt, mem_v_prompt, w, bp).reshape(bp, sp, d)
    k_prompt, v_prompt = _heads_view(kp, bp), _heads_view(vp, bp)

    past = page_table.shape[1] * PAGE_SIZE
    xs2 = x_sample.reshape(bd * td, d)
    zs2, ks, vs = _matmul(xs2, w_in_b, gain=g_mix, name="in_proj", head_major=(OFF_DK, OFF_DV))
    zs3 = zs2.reshape(bd, td, IN_W)
    ret_o_s, ret_state_sample = _retention(zs3, past + jnp.arange(td, dtype=jnp.int32), state_ret, td,
                                           "retention_sample")
    diff_o_s = _diff_sample(zs3, cache_k, cache_v, page_table, lams)
    h1s = _mixer_out(xs2, zs2, ret_o_s.reshape(bd * td, RET_W), diff_o_s.reshape(bd * td, DIFF_W), w)
    y_sample = _tail(h1s, cache_mem_k, cache_mem_v, w, bd).reshape(bd, td, d)
    k_sample, v_sample = _heads_view(ks, bd), _heads_view(vs, bd)

    return (y_prompt, y_sample, k_prompt, v_prompt, ret_state_prompt, mem_k_prompt, mem_v_prompt,
            k_sample, v_sample, ret_state_sample)
```
